```python
import jax
import jax.numpy as jnp
from jax import lax
import numpy as np

D_MODEL = 1024
BATCH = 4
SEQ = 4096
DEPTH = 2
DEC_BATCH = 32
DEC_SEQ = 4
PAST_LEN = 16384
PAGE_SIZE = 128

N_HEADS = 16
HEAD_DIM = D_MODEL // N_HEADS
D_FF = ((8 * D_MODEL // 3 + 127) // 128) * 128
CONV_WIDTH = 3
MOBA_BLOCK = 256
MOBA_TOPK = 3
MOBA_Q_CHUNK = 64
FOX_Q_BLOCK = 128
N_MIXERS = 2
N_MOBA_LAYERS = (DEPTH + 1) // 2
N_FOX_LAYERS = DEPTH // 2
FORGET_BIAS_INIT = 3.0
RMS_EPS = 1e-6

kernel_name = 'moba_fox_convffn_hybrid_step'


def rms_norm(x, gain):
    xf = x.astype(jnp.float32)
    y = xf * lax.rsqrt(jnp.mean(xf * xf, axis=-1, keepdims=True) + RMS_EPS)
    return (y * gain.astype(jnp.float32)).astype(x.dtype)


def alibi_slopes():
    return jnp.exp2(-8.0 * jnp.arange(1, N_HEADS + 1, dtype=jnp.float32) / N_HEADS)


def moba_project(h, w_qkv, q_gain, k_gain):
    b, n, _ = h.shape
    qkv = (h @ w_qkv).reshape(b, n, 3, N_HEADS, HEAD_DIM)
    return rms_norm(qkv[:, :, 0], q_gain), rms_norm(qkv[:, :, 1], k_gain), qkv[:, :, 2]


def fox_project(h, w_qkvf, b_f, q_gain, k_gain):
    b, n, _ = h.shape
    proj = h @ w_qkvf
    qkv = proj[..., :3 * D_MODEL].reshape(b, n, 3, N_HEADS, HEAD_DIM)
    log_f = jax.nn.log_sigmoid((proj[..., 3 * D_MODEL:] + b_f).astype(jnp.float32))
    return rms_norm(qkv[:, :, 0], q_gain), rms_norm(qkv[:, :, 1], k_gain), qkv[:, :, 2], log_f


def moba_attend(q, t, k_sel, v_sel, sel, slot_ok, k_own, v_own, own_pos, slopes):
    scale = HEAD_DIM ** -0.5
    tf = t.astype(jnp.float32)
    sel_pos = (sel[..., None] * MOBA_BLOCK + jnp.arange(MOBA_BLOCK)).astype(jnp.float32)
    s_sel = jnp.einsum('qhd,qhkld->qhkl', q, k_sel, preferred_element_type=jnp.float32) * scale
    s_sel = s_sel - slopes[None, :, None, None] * (tf[:, None, None, None] - sel_pos)
    s_sel = jnp.where(slot_ok[None, None, :, None], s_sel, -jnp.inf)
    s_own = jnp.einsum('qhd,hld->qhl', q, k_own, preferred_element_type=jnp.float32) * scale
    s_own = s_own - slopes[None, :, None] * (tf[:, None, None] - own_pos.astype(jnp.float32)[None, None, :])
    s_own = jnp.where(own_pos[None, None, :] <= t[:, None, None], s_own, -jnp.inf)
    n_q, n_h, n_k, blk = s_sel.shape
    p = jax.nn.softmax(jnp.concatenate([s_sel.reshape(n_q, n_h, n_k * blk), s_own], axis=-1), axis=-1)
    p_sel = p[..., :n_k * blk].reshape(n_q, n_h, n_k, blk).astype(v_sel.dtype)
    p_own = p[..., n_k * blk:].astype(v_own.dtype)
    out = (jnp.einsum('qhkl,qhkld->qhd', p_sel, v_sel, preferred_element_type=jnp.float32)
           + jnp.einsum('qhl,hld->qhd', p_own, v_own, preferred_element_type=jnp.float32))
    return out.astype(q.dtype)


def moba_prompt(q, k, v, slopes):
    b, n, h, dh = q.shape
    n_cand = max(-(-n // MOBA_BLOCK), MOBA_TOPK)
    pad = n_cand * MOBA_BLOCK - n

    def to_blocks(a):
        a = jnp.pad(a, ((0, 0), (0, pad), (0, 0), (0, 0)))
        return a.reshape(b, n_cand, MOBA_BLOCK, h, dh).transpose(0, 3, 1, 2, 4)

    kb = to_blocks(k)
    vb = to_blocks(v)
    k_mean = jnp.mean(kb.astype(jnp.float32), axis=3)
    n_chunks = n // MOBA_Q_CHUNK
    q_chunks = q.reshape(b * n_chunks, MOBA_Q_CHUNK, h, dh)
    flat = jnp.arange(b * n_chunks)
    b_ids = flat // n_chunks
    c_ids = flat % n_chunks
    head3 = jnp.arange(h)[None, :, None]
    slot_range = jnp.arange(MOBA_TOPK)
    cand_range = jnp.arange(n_cand)

    def chunk(args):
        qc, bi, ci = args
        t0 = ci * MOBA_Q_CHUNK
        t = t0 + jnp.arange(MOBA_Q_CHUNK)
        own = t0 // MOBA_BLOCK
        gate = jnp.einsum('qhd,hnd->qhn', qc.astype(jnp.float32), k_mean[bi])
        gate = jnp.where(cand_range < own, gate, -jnp.inf)
        _, sel = lax.top_k(gate, MOBA_TOPK)
        k_sel = kb[bi, head3, sel]
        v_sel = vb[bi, head3, sel]
        k_own = kb[bi, :, own]
        v_own = vb[bi, :, own]
        own_pos = own * MOBA_BLOCK + jnp.arange(MOBA_BLOCK)
        return moba_attend(qc, t, k_sel, v_sel, sel, slot_range < own, k_own, v_own, own_pos, slopes)

    out = lax.map(chunk, (q_chunks, b_ids, c_ids))
    return out.reshape(b, n, h, dh)


def moba_sample(q, k_new, v_new, cache_k, cache_v, layer, page_table, slopes):
    t_new, h, dh = q.shape[1:]
    n_pages = page_table.shape[1]
    past = n_pages * PAGE_SIZE
    ppb = MOBA_BLOCK // PAGE_SIZE
    own_start = (past // MOBA_BLOCK) * MOBA_BLOCK
    n_full = own_start // MOBA_BLOCK
    n_cand = max(n_full, MOBA_TOPK)
    n_own_cached = past - own_start
    first_own_page = own_start // PAGE_SIZE
    t = past + jnp.arange(t_new)
    own_pos = own_start + jnp.arange(n_own_cached + t_new)
    slot_ok = jnp.arange(MOBA_TOPK) < n_full
    cand_ok = jnp.arange(n_cand) < n_full
    head3 = jnp.arange(h)[None, :, None]
    head5 = jnp.arange(h)[None, :, None, None, None]
    rows = jnp.arange(PAGE_SIZE)

    def seq_fn(args):
        qs, kn, vn, pt = args
        k_pages = cache_k[layer, pt]
        k_blocks = k_pages[:n_full * ppb].reshape(n_full, MOBA_BLOCK, h, dh).transpose(2, 0, 1, 3)
        k_blocks = jnp.pad(k_blocks, ((0, 0), (0, n_cand - n_full), (0, 0), (0, 0)))
        k_mean = jnp.mean(k_blocks.astype(jnp.float32), axis=2)
        gate = jnp.einsum('thd,hnd->thn', qs.astype(jnp.float32), k_mean)
        gate = jnp.where(cand_ok, gate, -jnp.inf)
        _, sel = lax.top_k(gate, MOBA_TOPK)
        k_sel = k_blocks[head3, sel]
        logical = jnp.clip(sel[..., None] * ppb + jnp.arange(ppb), 0, n_pages - 1)
        phys = pt[logical]
        v_sel = cache_v[layer, phys[..., None], rows, head5].reshape(t_new, h, MOBA_TOPK, MOBA_BLOCK, dh)
        k_own_past = k_pages[first_own_page:].reshape(n_own_cached, h, dh)
        v_own_past = cache_v[layer, pt[first_own_page:]].reshape(n_own_cached, h, dh)
        k_own = jnp.concatenate([k_own_past, kn], axis=0).transpose(1, 0, 2)
        v_own = jnp.concatenate([v_own_past, vn], axis=0).transpose(1, 0, 2)
        return moba_attend(qs, t, k_sel, v_sel, sel, slot_ok, k_own, v_own, own_pos, slopes)

    return lax.map(seq_fn, (q, k_new, v_new, page_table))


def fox_prompt(q, k, v, log_f):
    b, n, h, dh = q.shape
    scale = dh ** -0.5
    c = jnp.cumsum(log_f, axis=1).transpose(0, 2, 1)
    n_blk = n // FOX_Q_BLOCK
    q_blk = q.reshape(b, n_blk, FOX_Q_BLOCK, h, dh).transpose(1, 0, 2, 3, 4)
    c_blk = c.reshape(b, h, n_blk, FOX_Q_BLOCK).transpose(2, 0, 1, 3)
    k_pos = jnp.arange(n)

    def block(args):
        qb, cb, i = args
        t = i * FOX_Q_BLOCK + jnp.arange(FOX_Q_BLOCK)
        s = jnp.einsum('bqhd,bkhd->bhqk', qb, k, preferred_element_type=jnp.float32) * scale
        s = s + cb[..., :, None] - c[:, :, None, :]
        s = jnp.where(k_pos[None, :] <= t[:, None], s, -jnp.inf)
        p = jax.nn.softmax(s, axis=-1)
        o = jnp.einsum('bhqk,bkhd->bqhd', p.astype(v.dtype), v, preferred_element_type=jnp.float32)
        return o.astype(q.dtype)

    out = lax.map(block, (q_blk, c_blk, jnp.arange(n_blk)))
    return out.transpose(1, 0, 2, 3, 4).reshape(b, n, h, dh)


def fox_sample(q, k_new, v_new, lf_new, cache_k, cache_v, cache_lf, layer, page_table):
    t_new, h, dh = q.shape[1:]
    past = page_table.shape[1] * PAGE_SIZE
    scale = dh ** -0.5
    ar = jnp.arange(t_new)
    mask = jnp.concatenate([jnp.ones((t_new, past), dtype=bool), ar[None, :] <= ar[:, None]], axis=1)

    def seq_fn(args):
        qs, kn, vn, lfn, pt = args
        k_all = jnp.concatenate([cache_k[layer, pt].reshape(past, h, dh), kn], axis=0)
        v_all = jnp.concatenate([cache_v[layer, pt].reshape(past, h, dh), vn], axis=0)
        lf_past = cache_lf[layer, pt].reshape(past, h).astype(jnp.float32)
        suffix = lax.cumsum(lf_past, axis=0, reverse=True) - lf_past
        c_new = jnp.cumsum(lfn, axis=0)
        decay = jnp.concatenate([suffix[None, :, :] + c_new[:, None, :],
                                 c_new[:, None, :] - c_new[None, :, :]], axis=1)
        s = jnp.einsum('thd,shd->hts', qs, k_all, preferred_element_type=jnp.float32) * scale
        s = s + decay.transpose(2, 0, 1)
        s = jnp.where(mask[None], s, -jnp.inf)
        p = jax.nn.softmax(s, axis=-1)
        o = jnp.einsum('hts,shd->thd', p.astype(v_all.dtype), v_all, preferred_element_type=jnp.float32)
        return o.astype(qs.dtype)

    return lax.map(seq_fn, (q, k_new, v_new, lf_new, page_table))


def conv_ffn(x, g_hist, norm_gain, w_gate, w_up, conv_w, conv_b, w_down):
    n = x.shape[1]
    xn = rms_norm(x, norm_gain)
    g = xn @ w_gate
    u = xn @ w_up
    g_ext = jnp.concatenate([g_hist.astype(g.dtype), g], axis=1)
    g_conv = conv_b + conv_w[0] * g_ext[:, 0:n]
    for i in range(1, CONV_WIDTH):
        g_conv = g_conv + conv_w[i] * g_ext[:, i:i + n]
    hidden = jax.nn.silu(g_conv) * u
    return x + hidden @ w_down, g_ext[:, n:]


def setup_inputs(seed: int = 0) -> dict:
    key = jax.random.key(seed)
    keys = iter(jax.random.split(key, 32))

    def nrm(shape, scale=1.0):
        a = jax.random.normal(next(keys), shape, jnp.float32)
        return a if scale == 1.0 else a * scale

    n_pages = PAST_LEN // PAGE_SIZE
    n_used = DEC_BATCH * n_pages
    n_phys = n_used + max(1, n_used // 4)
    kv_moba = (N_MOBA_LAYERS, n_phys, PAGE_SIZE, N_HEADS, HEAD_DIM)
    kv_fox = (N_FOX_LAYERS, n_phys, PAGE_SIZE, N_HEADS, HEAD_DIM)
    d_in = D_MODEL ** -0.5
    return {
        'x_prompt': nrm((BATCH, SEQ, D_MODEL)),
        'x_sample': nrm((DEC_BATCH, DEC_SEQ, D_MODEL)),
        'cache_k_moba': nrm(kv_moba),
        'cache_v_moba': nrm(kv_moba),
        'cache_k_fox': nrm(kv_fox),
        'cache_v_fox': nrm(kv_fox),
        'cache_logf_fox': jax.nn.log_sigmoid(FORGET_BIAS_INIT + nrm((N_FOX_LAYERS, n_phys, PAGE_SIZE, N_HEADS))),
        'state_conv': nrm((DEPTH, DEC_BATCH, CONV_WIDTH - 1, D_FF)),
        'page_table': jax.random.permutation(next(keys), n_phys)[:n_used].reshape(DEC_BATCH, n_pages).astype(jnp.int32),
        'attn_norm': 1.0 + nrm((DEPTH, D_MODEL), 0.02),
        'ffn_norm': 1.0 + nrm((DEPTH, D_MODEL), 0.02),
        'moba_w_qkv': nrm((N_MOBA_LAYERS, D_MODEL, 3 * D_MODEL), d_in),
        'moba_q_gain': 1.0 + nrm((N_MOBA_LAYERS, HEAD_DIM), 0.02),
        'moba_k_gain': 1.0 + nrm((N_MOBA_LAYERS, HEAD_DIM), 0.02),
        'moba_w_o': nrm((N_MOBA_LAYERS, D_MODEL, D_MODEL), d_in),
        'fox_w_qkvf': nrm((N_FOX_LAYERS, D_MODEL, 3 * D_MODEL + N_HEADS), d_in),
        'fox_b_f': FORGET_BIAS_INIT + nrm((N_FOX_LAYERS, N_HEADS), 0.5),
        'fox_q_gain': 1.0 + nrm((N_FOX_LAYERS, HEAD_DIM), 0.02),
        'fox_k_gain': 1.0 + nrm((N_FOX_LAYERS, HEAD_DIM), 0.02),
        'fox_w_o': nrm((N_FOX_LAYERS, D_MODEL, D_MODEL), d_in),
        'ffn_w_gate': nrm((DEPTH, D_MODEL, D_FF), d_in),
        'ffn_w_up': nrm((DEPTH, D_MODEL, D_FF), d_in),
        'ffn_conv_w': nrm((DEPTH, CONV_WIDTH, D_FF), CONV_WIDTH ** -0.5),
        'ffn_conv_b': nrm((DEPTH, D_FF), 0.01),
        'ffn_w_down': nrm((DEPTH, D_FF, D_MODEL), D_FF ** -0.5),
    }


def reference(x_prompt, x_sample, cache_k_moba, cache_v_moba, cache_k_fox, cache_v_fox,
              cache_logf_fox, state_conv, page_table, attn_norm, ffn_norm,
              moba_w_qkv, moba_q_gain, moba_k_gain, moba_w_o,
              fox_w_qkvf, fox_b_f, fox_q_gain, fox_k_gain, fox_w_o,
              ffn_w_gate, ffn_w_up, ffn_conv_w, ffn_conv_b, ffn_w_down):
    slopes = alibi_slopes()
    b, n, d = x_prompt.shape
    db, t_new, _ = x_sample.shape
    y_prompt, y_sample = x_prompt, x_sample
    moba_kp, moba_vp, moba_ks, moba_vs = [], [], [], []
    fox_kp, fox_vp, fox_lp, fox_ks, fox_vs, fox_ls = [], [], [], [], [], []
    conv_p, conv_s = [], []
    for layer in range(DEPTH):
        j = layer // N_MIXERS
        hp = rms_norm(y_prompt, attn_norm[layer])
        hs = rms_norm(y_sample, attn_norm[layer])
        if layer % N_MIXERS == 0:
            qp, kp, vp = moba_project(hp, moba_w_qkv[j], moba_q_gain[j], moba_k_gain[j])
            qs, ks, vs = moba_project(hs, moba_w_qkv[j], moba_q_gain[j], moba_k_gain[j])
            mix_p = moba_prompt(qp, kp, vp, slopes)
            mix_s = moba_sample(qs, ks, vs, cache_k_moba, cache_v_moba, j, page_table, slopes)
            w_o = moba_w_o[j]
            moba_kp.append(kp)
            moba_vp.append(vp)
            moba_ks.append(ks)
            moba_vs.append(vs)
        else:
            qp, kp, vp, lp = fox_project(hp, fox_w_qkvf[j], fox_b_f[j], fox_q_gain[j], fox_k_gain[j])
            qs, ks, vs, ls = fox_project(hs, fox_w_qkvf[j], fox_b_f[j], fox_q_gain[j], fox_k_gain[j])
            mix_p = fox_prompt(qp, kp, vp, lp)
            mix_s = fox_sample(qs, ks, vs, ls, cache_k_fox, cache_v_fox, cache_logf_fox, j, page_table)
            w_o = fox_w_o[j]
            fox_kp.append(kp)
            fox_vp.append(vp)
            fox_lp.append(lp)
            fox_ks.append(ks)
            fox_vs.append(vs)
            fox_ls.append(ls)
        y_prompt = y_prompt + mix_p.reshape(b, n, d) @ w_o
        y_sample = y_sample + mix_s.reshape(db, t_new, d) @ w_o
        y_prompt, cp = conv_ffn(y_prompt, jnp.zeros((b, CONV_WIDTH - 1, D_FF), y_prompt.dtype),
                                ffn_norm[layer], ffn_w_gate[layer], ffn_w_up[layer],
                                ffn_conv_w[layer], ffn_conv_b[layer], ffn_w_down[layer])
        y_sample, cs = conv_ffn(y_sample, state_conv[layer],
                                ffn_norm[layer], ffn_w_gate[layer], ffn_w_up[layer],
                                ffn_conv_w[layer], ffn_conv_b[layer], ffn_w_down[layer])
        conv_p.append(cp)
        conv_s.append(cs)
    new_k_moba_prompt = jnp.stack(moba_kp)
    new_v_moba_prompt = jnp.stack(moba_vp)
    new_k_moba_sample = jnp.stack(moba_ks)
    new_v_moba_sample = jnp.stack(moba_vs)
    new_k_fox_prompt = jnp.stack(fox_kp)
    new_v_fox_prompt = jnp.stack(fox_vp)
    new_logf_fox_prompt = jnp.stack(fox_lp)
    new_k_fox_sample = jnp.stack(fox_ks)
    new_v_fox_sample = jnp.stack(fox_vs)
    new_logf_fox_sample = jnp.stack(fox_ls)
    new_conv_prompt = jnp.stack(conv_p)
    new_conv_sample = jnp.stack(conv_s)
    return (y_prompt, y_sample,
            new_k_moba_prompt, new_v_moba_prompt, new_k_moba_sample, new_v_moba_sample,
            new_k_fox_prompt, new_v_fox_prompt, new_logf_fox_prompt,
            new_k_fox_sample, new_v_fox_sample, new_logf_fox_sample,
            new_conv_prompt, new_conv_sample)
```

```python
import functools

import numpy as np
import jax
import jax.numpy as jnp
from jax import lax
from jax.experimental import pallas as pl
from jax.experimental.pallas import tpu as pltpu

N_HEADS = 16
HEAD_DIM = 64
D_MODEL = N_HEADS * HEAD_DIM
MOBA_BLOCK = 256
MOBA_TOPK = 3
PAGE_SIZE = 128
CONV_WIDTH = 3
RMS_EPS = 1e-6
LANES = 128
SUBLANES = 8
HEADS_PER_TILE = LANES // HEAD_DIM
VMEM_LIMIT = 56 * 1024 * 1024
NEG = -1e30
ATTN_TILE = MOBA_BLOCK

F32 = jnp.float32
BF16 = jnp.bfloat16

_SLOPES = [float(s) for s in np.exp2(np.float32(-8.0) * np.arange(1, N_HEADS + 1, dtype=np.float32)
                                     / np.float32(N_HEADS)).astype(np.float32)]


def _dot(a, b):
    return jnp.dot(a, b, preferred_element_type=F32)


def _dot_nt(a, b):
    return lax.dot_general(a, b, (((1,), (1,)), ((), ())), preferred_element_type=F32)


def _split(a, terms):
    parts = []
    r = a
    for _ in range(terms):
        p = r.astype(BF16)
        parts.append(p)
        r = r - p.astype(F32)
    return parts


def _dot_split_lhs(a, b_bf16, terms):
    out = None
    for p in _split(a, terms):
        d = _dot(p, b_bf16)
        out = d if out is None else out + d
    return out


def _params(sem):
    return pltpu.CompilerParams(dimension_semantics=sem, vmem_limit_bytes=VMEM_LIMIT)


def _resident(shape):
    return pl.BlockSpec(shape, lambda *_: (0,) * len(shape), pipeline_mode=pl.Buffered(1))


def _head_of_lane(shape, dim):
    return lax.broadcasted_iota(jnp.int32, shape, dim) // HEAD_DIM


def _qkv_body(*refs, fox, kmean):
    x_ref, g_ref, w_ref, qg_ref, kg_ref, seg_ref, segt_ref = refs[:7]
    pos = 7
    if fox:
        wf_ref, bf_ref = refs[pos:pos + 2]
        pos += 2
    q_ref, k_ref, v_ref = refs[pos:pos + 3]
    pos += 3
    if fox:
        lf_ref = refs[pos]
        pos += 1
    if kmean:
        km_ref = refs[pos]

    x = x_ref[...]
    ms = jnp.mean(x * x, axis=-1, keepdims=True)
    h = ((x * lax.rsqrt(ms + RMS_EPS)) * g_ref[...]).astype(BF16)
    qkv = _dot(h, w_ref[...])
    seg = seg_ref[...]
    segt = segt_ref[...]

    def head_norm(a, gain):
        ss = _dot_split_lhs(a * a, seg, 2)
        rs = lax.rsqrt(ss * (1.0 / HEAD_DIM) + RMS_EPS)
        scale = _dot_split_lhs(rs, segt, 2)
        return (a * scale) * gain

    q = head_norm(qkv[:, :D_MODEL], qg_ref[...])
    k = head_norm(qkv[:, D_MODEL:2 * D_MODEL], kg_ref[...])
    q_ref[...] = q
    k_ref[...] = k
    v_ref[...] = qkv[:, 2 * D_MODEL:]
    if fox:
        f = _dot(h, wf_ref[...]) + bf_ref[...]
        lf = -(jnp.maximum(-f, 0.0) + jnp.log1p(jnp.exp(-jnp.abs(f))))
        lf_ref[...] = lf[:, :N_HEADS]
    if kmean:
        km_ref[0] = jnp.sum(k, axis=0, keepdims=True) * (1.0 / MOBA_BLOCK)


def _seg_mats():
    lane_head = np.arange(D_MODEL) // HEAD_DIM
    seg = (lane_head[:, None] == np.arange(LANES)[None, :]).astype(np.float32)
    return jnp.asarray(seg, BF16), jnp.asarray(seg.T, BF16)


def _qkv_proj(x, gain, w, q_gain, k_gain, wf=None, bf=None, *, kmean=False):
    n = x.shape[0]
    tm = min(MOBA_BLOCK, n)
    fox = wf is not None
    seg, segt = _seg_mats()
    row = lambda i: (i, 0)
    in_specs = [pl.BlockSpec((tm, D_MODEL), row),
                _resident((1, D_MODEL)),
                _resident((D_MODEL, 3 * D_MODEL)),
                _resident((1, D_MODEL)),
                _resident((1, D_MODEL)),
                _resident((D_MODEL, LANES)),
                _resident((LANES, D_MODEL))]
    args = [x, gain.reshape(1, D_MODEL), w,
            jnp.tile(q_gain, N_HEADS).reshape(1, D_MODEL), jnp.tile(k_gain, N_HEADS).reshape(1, D_MODEL),
            seg, segt]
    if fox:
        in_specs += [_resident((D_MODEL, LANES)), _resident((1, LANES))]
        args += [wf, bf]
    out_shape = [jax.ShapeDtypeStruct((n, D_MODEL), F32)] * 3
    out_specs = [pl.BlockSpec((tm, D_MODEL), row)] * 3
    if fox:
        out_shape.append(jax.ShapeDtypeStruct((n, N_HEADS), F32))
        out_specs.append(pl.BlockSpec((tm, N_HEADS), row))
    if kmean:
        out_shape.append(jax.ShapeDtypeStruct((n // tm, 1, D_MODEL), F32))
        out_specs.append(pl.BlockSpec((1, 1, D_MODEL), lambda i: (i, 0, 0)))
    return pl.pallas_call(
        functools.partial(_qkv_body, fox=fox, kmean=kmean),
        grid=(n // tm,),
        in_specs=in_specs, out_specs=out_specs, out_shape=out_shape,
        compiler_params=_params(("parallel",)),
        name="qkv_proj",
    )(*args)


def _wo_body(x_ref, a_ref, w_ref, o_ref):
    o_ref[...] = x_ref[...] + _dot(a_ref[...].astype(BF16), w_ref[...])


def _wo_proj(x, a, w):
    n = x.shape[0]
    tm = min(512, n)
    row = lambda i: (i, 0)
    return pl.pallas_call(
        _wo_body,
        grid=(n // tm,),
        in_specs=[pl.BlockSpec((tm, D_MODEL), row), pl.BlockSpec((tm, D_MODEL), row),
                  _resident((D_MODEL, D_MODEL))],
        out_specs=pl.BlockSpec((tm, D_MODEL), row),
        out_shape=jax.ShapeDtypeStruct((n, D_MODEL), F32),
        compiler_params=_params(("parallel",)),
        name="wo_proj",
    )(x, a, w)


def _ffn_body(*refs, tm, sample, tiles_per_seq):
    if sample:
        (x_ref, g_ref, wg_ref, wu_ref, cw_ref, cb_ref, wd_ref, h1_ref, h2_ref,
         y_ref, gout_ref, gbuf) = refs
    else:
        (x_ref, g_ref, wg_ref, wu_ref, cw_ref, cb_ref, wd_ref, hist_ref,
         y_ref, gout_ref, gbuf) = refs
    halo = SUBLANES
    x = x_ref[...]
    ms = jnp.mean(x * x, axis=-1, keepdims=True)
    xn = ((x * lax.rsqrt(ms + RMS_EPS)) * g_ref[...]).astype(BF16)
    g = _dot(xn, wg_ref[...])
    u = _dot(xn, wu_ref[...])
    if sample:
        gbuf[0:halo, :] = jnp.zeros((halo, g.shape[1]), F32)
    else:
        i = pl.program_id(0)

        @pl.when(i % tiles_per_seq != 0)
        def _():
            gbuf[0:halo, :] = gbuf[tm:tm + halo, :]

        @pl.when(i % tiles_per_seq == 0)
        def _():
            gbuf[halo - (CONV_WIDTH - 1):halo, :] = hist_ref[...]
    gbuf[halo:halo + tm, :] = g
    prev1 = gbuf[halo - 1:halo - 1 + tm, :]
    prev2 = gbuf[halo - 2:halo - 2 + tm, :]
    if sample:
        t = lax.broadcasted_iota(jnp.int32, (tm, 1), 0) % tiles_per_seq
        prev1 = jnp.where(t == 0, h1_ref[...], prev1)
        prev2 = jnp.where(t < 2, h2_ref[...], prev2)
        gout_ref[...] = g
    else:
        gout_ref[...] = g[tm - (CONV_WIDTH - 1):, :]
    cw = cw_ref[...]
    g_conv = cb_ref[...] + cw[0:1, :] * prev2
    g_conv = g_conv + cw[1:2, :] * prev1
    g_conv = g_conv + cw[2:3, :] * g
    hidden = (g_conv * jax.nn.sigmoid(g_conv)) * u
    y_ref[...] = x + _dot(hidden.astype(BF16), wd_ref[...])


def _conv_ffn(x, hist, gain, wg, wu, cw, cb, wd, *, rows_per_seq):
    n = x.shape[0]
    f = wg.shape[1]
    n_seq = n // rows_per_seq
    sample = rows_per_seq < SUBLANES
    row = lambda i: (i, 0)
    w_specs = [_resident((1, D_MODEL)), _resident((D_MODEL, f)), _resident((D_MODEL, f)),
               _resident((CONV_WIDTH, f)), _resident((1, f)), _resident((f, D_MODEL))]
    w_args = [gain.reshape(1, D_MODEL), wg, wu, cw, cb.reshape(1, f), wd]
    if sample:
        tm = n
        t = jnp.arange(n) % rows_per_seq
        h_rep = jnp.repeat(hist, rows_per_seq, axis=0)
        h1 = h_rep[:, 1]
        h2 = jnp.where((t == 0)[:, None], h_rep[:, 0], h_rep[:, 1])
        y, g = pl.pallas_call(
            functools.partial(_ffn_body, tm=tm, sample=True, tiles_per_seq=rows_per_seq),
            grid=(1,),
            in_specs=[pl.BlockSpec((tm, D_MODEL), row)] + w_specs
                     + [pl.BlockSpec((tm, f), row), pl.BlockSpec((tm, f), row)],
            out_specs=[pl.BlockSpec((tm, D_MODEL), row), pl.BlockSpec((tm, f), row)],
            out_shape=[jax.ShapeDtypeStruct((n, D_MODEL), F32), jax.ShapeDtypeStruct((n, f), F32)],
            scratch_shapes=[pltpu.VMEM((tm + SUBLANES, f), F32)],
            compiler_params=_params(("arbitrary",)),
            name="conv_ffn_sample",
        )(x, *w_args, h1, h2)
        return y, g.reshape(n_seq, rows_per_seq, f)[:, rows_per_seq - (CONV_WIDTH - 1):]
    tm = min(256, rows_per_seq)
    tiles_per_seq = rows_per_seq // tm
    seq_blk = lambda i: (i // tiles_per_seq, 0, 0)
    y, new_state = pl.pallas_call(
        functools.partial(_ffn_body, tm=tm, sample=False, tiles_per_seq=tiles_per_seq),
        grid=(n // tm,),
        in_specs=[pl.BlockSpec((tm, D_MODEL), row)] + w_specs
                 + [pl.BlockSpec((None, CONV_WIDTH - 1, f), seq_blk)],
        out_specs=[pl.BlockSpec((tm, D_MODEL), row), pl.BlockSpec((None, CONV_WIDTH - 1, f), seq_blk)],
        out_shape=[jax.ShapeDtypeStruct((n, D_MODEL), F32),
                   jax.ShapeDtypeStruct((n_seq, CONV_WIDTH - 1, f), F32)],
        scratch_shapes=[pltpu.VMEM((tm + SUBLANES, f), F32)],
        compiler_params=_params(("arbitrary",)),
        name="conv_ffn_prompt",
    )(x, *w_args, hist)
    return y, new_state


def _cumsum_body(lf_ref, c_ref, carry, *, tm, tiles_per_seq):
    i = pl.program_id(0)

    @pl.when(i % tiles_per_seq == 0)
    def _():
        carry[...] = jnp.zeros_like(carry)

    r = lax.broadcasted_iota(jnp.int32, (tm, tm), 0)
    c = lax.broadcasted_iota(jnp.int32, (tm, tm), 1)
    tri = jnp.where(c <= r, 1.0, 0.0).astype(BF16)
    out = None
    for p in _split(lf_ref[...], 3):
        d = _dot(tri, p)
        out = d if out is None else out + d
    out = out + carry[...]
    c_ref[...] = out
    carry[...] = out[tm - 1:tm, :]


def _seq_cumsum(lf, rows_per_seq):
    n = lf.shape[0]
    tm = min(512, rows_per_seq)
    return pl.pallas_call(
        functools.partial(_cumsum_body, tm=tm, tiles_per_seq=rows_per_seq // tm),
        grid=(n // tm,),
        in_specs=[pl.BlockSpec((tm, N_HEADS), lambda i: (i, 0))],
        out_specs=pl.BlockSpec((tm, N_HEADS), lambda i: (i, 0)),
        out_shape=jax.ShapeDtypeStruct((n, N_HEADS), F32),
        scratch_shapes=[pltpu.VMEM((1, N_HEADS), F32)],
        compiler_params=_params(("arbitrary",)),
        name="logf_cumsum",
    )(lf)


def _top_k_mask(gate, k, n_valid):
    lane = lax.broadcasted_iota(jnp.int32, gate.shape, 1)
    sel = jnp.zeros(gate.shape, jnp.bool_)
    for r in range(k):
        mx = jnp.max(gate, axis=-1, keepdims=True)
        idx = jnp.min(jnp.where(gate == mx, lane, LANES), axis=-1, keepdims=True)
        hit = lane == idx
        sel = jnp.logical_or(sel, jnp.logical_and(hit, r < n_valid))
        gate = jnp.where(hit, -jnp.inf, gate)
    return sel


def _prompt_attn_body(*refs, fox):
    qi_ref, kn_ref = refs[:2]
    if fox:
        q_ref, k_ref, v_ref, c_ref, ct_ref, o_ref, qa, m_scr, l_scr, acc = refs[2:]
    else:
        q_ref, k_ref, v_ref, km_ref, o_ref, qa, m_scr, l_scr, acc = refs[2:]
    t = ATTN_TILE
    step = pl.program_id(1)
    i = qi_ref[step]
    n = kn_ref[step]
    scale = HEAD_DIM ** -0.5
    lane = lax.broadcasted_iota(jnp.int32, (t, LANES), 1)
    lane_head = lane // HEAD_DIM

    @pl.when(n == 0)
    def _():
        m_scr[...] = jnp.full(m_scr.shape, NEG, F32)
        l_scr[...] = jnp.zeros(l_scr.shape, F32)
        acc[...] = jnp.zeros(acc.shape, F32)
        for h in range(N_HEADS):
            hp, e = divmod(h, HEADS_PER_TILE)
            q2 = q_ref[:, hp * LANES:(hp + 1) * LANES]
            qh = jnp.where(lane_head == e, q2, 0.0)
            if fox:
                qa[h] = (qh * scale).astype(BF16)
            else:
                km2 = km_ref[:, hp * LANES:(hp + 1) * LANES]
                km_pad = jnp.concatenate([km2, jnp.zeros((LANES - km2.shape[0], LANES), F32)], axis=0)
                q_hi, q_lo = _split(qh, 2)
                k_hi, k_lo = _split(km_pad, 2)
                gate = _dot_nt(q_hi, k_hi) + (_dot_nt(q_hi, k_lo) + _dot_nt(q_lo, k_hi))
                gate = jnp.where(lane < i, gate, -jnp.inf)
                sel = _top_k_mask(gate, MOBA_TOPK, i)
                sel_bias = jnp.where(jnp.logical_or(sel, lane >= i), 0.0, NEG)
                qa[h] = jnp.concatenate([(qh * scale).astype(BF16), sel_bias.astype(BF16)], axis=-1)

    row = lax.broadcasted_iota(jnp.int32, (t, t), 0)
    col = lax.broadcasted_iota(jnp.int32, (t, t), 1)
    dist = (row - col + (i - n) * t).astype(F32)
    allow = dist >= 0.0
    if not fox:
        block_onehot = jnp.where(lane == n, 1.0, 0.0).astype(BF16)

    for hp in range(N_HEADS // HEADS_PER_TILE):
        sl = slice(hp * LANES, (hp + 1) * LANES)
        k2 = k_ref[:, sl].astype(BF16)
        v2 = v_ref[:, sl].astype(BF16)
        if not fox:
            k2 = jnp.concatenate([k2, block_onehot], axis=-1)
        alphas, pvs = [], []
        for e in range(HEADS_PER_TILE):
            h = hp * HEADS_PER_TILE + e
            s = _dot_nt(qa[h], k2)
            if fox:
                s = s + (c_ref[:, h:h + 1] - ct_ref[h:h + 1, :])
            else:
                s = s - _SLOPES[h] * dist
            s = jnp.where(allow, s, NEG)
            m_prev = m_scr[h]
            m_new = jnp.maximum(m_prev, jnp.max(s, axis=-1, keepdims=True))
            alpha = jnp.exp(m_prev - m_new)
            p = jnp.exp(s - m_new)
            l_scr[h] = alpha * l_scr[h] + jnp.sum(p, axis=-1, keepdims=True)
            m_scr[h] = m_new
            alphas.append(alpha)
            pvs.append(_dot(p.astype(BF16), v2))
        first = lane_head == 0
        acc[:, sl] = acc[:, sl] * jnp.where(first, alphas[0], alphas[1]) + jnp.where(first, pvs[0], pvs[1])

    @pl.when(n == i)
    def _():
        for hp in range(N_HEADS // HEADS_PER_TILE):
            sl = slice(hp * LANES, (hp + 1) * LANES)
            l2 = jnp.where(lane_head == 0, l_scr[hp * HEADS_PER_TILE], l_scr[hp * HEADS_PER_TILE + 1])
            o_ref[:, sl] = acc[:, sl] / l2


def _prompt_attn(q, k, v, *, batch, kmean=None, c=None):
    n = q.shape[0]
    t = ATTN_TILE
    nq = n // batch // t
    fox = c is not None
    pairs = [(i, j) for i in range(nq) for j in range(i + 1)]
    qi = jnp.asarray([p[0] for p in pairs], jnp.int32)
    kn = jnp.asarray([p[1] for p in pairs], jnp.int32)
    q_map = lambda b, s, qi, kn: (b * nq + qi[s], 0)
    k_map = lambda b, s, qi, kn: (b * nq + kn[s], 0)
    in_specs = [pl.BlockSpec((t, D_MODEL), q_map), pl.BlockSpec((t, D_MODEL), k_map),
                pl.BlockSpec((t, D_MODEL), k_map)]
    args = [q, k, v]
    if fox:
        in_specs += [pl.BlockSpec((t, N_HEADS), q_map),
                     pl.BlockSpec((N_HEADS, t), lambda b, s, qi, kn: (0, b * nq + kn[s]))]
        args += [c, c.T]
        qa_shape = (N_HEADS, t, LANES)
    else:
        in_specs += [pl.BlockSpec((None, kmean.shape[1], D_MODEL), lambda b, s, qi, kn: (b, 0, 0))]
        args += [kmean]
        qa_shape = (N_HEADS, t, 2 * LANES)
    return pl.pallas_call(
        functools.partial(_prompt_attn_body, fox=fox),
        grid_spec=pltpu.PrefetchScalarGridSpec(
            num_scalar_prefetch=2,
            grid=(batch, len(pairs)),
            in_specs=in_specs,
            out_specs=pl.BlockSpec((t, D_MODEL), q_map),
            scratch_shapes=[pltpu.VMEM(qa_shape, BF16),
                            pltpu.VMEM((N_HEADS, t, 1), F32),
                            pltpu.VMEM((N_HEADS, t, 1), F32),
                            pltpu.VMEM((t, D_MODEL), F32)]),
        out_shape=jax.ShapeDtypeStruct((n, D_MODEL), F32),
        compiler_params=_params(("parallel", "arbitrary")),
        name="fox_prompt_attn" if fox else "moba_prompt_attn",
    )(qi, kn, *args)


def _sample_attn_body(*refs, fox, t_new, n_pages):
    pt_ref = refs[0]
    if fox:
        (q_ref, kn_ref, vn_ref, ck_ref, cv_ref, clf_ref, lfn_ref, o_ref,
         qx, kpad, vpad, s_scr, acc, l_scr, carry, lfpad) = refs[1:]
    else:
        (q_ref, kn_ref, vn_ref, ck_ref, cv_ref, o_ref,
         qx, kpad, vpad, s_scr, acc, l_scr, ksum, km_scr) = refs[1:]
    del pt_ref
    rows = t_new * N_HEADS
    past = n_pages * PAGE_SIZE
    phase = pl.program_id(1)
    p = pl.program_id(2)
    scale = HEAD_DIM ** -0.5
    row_col = lax.broadcasted_iota(jnp.int32, (rows, 1), 0)
    tok_col = row_col // N_HEADS
    head_col = row_col % N_HEADS

    @pl.when(jnp.logical_and(phase == 0, p == 0))
    def _():
        head_mask = _head_of_lane((N_HEADS, D_MODEL), 1) == lax.broadcasted_iota(jnp.int32, (N_HEADS, D_MODEL), 0)
        for t in range(t_new):
            qrow = jnp.broadcast_to(q_ref[t:t + 1, :], (N_HEADS, D_MODEL))
            qx[t * N_HEADS:(t + 1) * N_HEADS, :] = jnp.where(head_mask, qrow * scale, 0.0)
        kpad[...] = jnp.zeros(kpad.shape, BF16)
        vpad[...] = jnp.zeros(vpad.shape, BF16)
        kpad[0:t_new, :] = kn_ref[...].astype(BF16)
        vpad[0:t_new, :] = vn_ref[...].astype(BF16)
        s_scr[:, past:past + PAGE_SIZE] = _dot_nt(qx[...].astype(BF16), kpad[...])
        if fox:
            carry[...] = jnp.zeros(carry.shape, F32)
            lfpad[...] = jnp.zeros(lfpad.shape, F32)
        else:
            km_scr[...] = jnp.zeros(km_scr.shape, F32)

    @pl.when(phase == 0)
    def _():
        pk = n_pages - 1 - p
        kpage = ck_ref[...]
        s = _dot_nt(qx[...].astype(BF16), kpage.astype(BF16))
        if fox:
            lfpad[:, 0:N_HEADS] = clf_ref[...]
            lft = lfpad[...].T[0:N_HEADS, :]
            r0 = lax.broadcasted_iota(jnp.int32, (PAGE_SIZE, PAGE_SIZE), 0)
            r1 = lax.broadcasted_iota(jnp.int32, (PAGE_SIZE, PAGE_SIZE), 1)
            later = jnp.where(r0 > r1, 1.0, 0.0).astype(BF16)
            suffix = _dot_split_lhs(lft, later, 3) + carry[...]
            carry[...] = carry[...] + jnp.sum(lft, axis=-1, keepdims=True)
            s = s + jnp.concatenate([suffix] * t_new, axis=0)
        else:
            pages_per_block = MOBA_BLOCK // PAGE_SIZE
            colsum = jnp.sum(kpage, axis=0, keepdims=True)

            @pl.when(pk % pages_per_block == pages_per_block - 1)
            def _():
                ksum[...] = colsum

            @pl.when(pk % pages_per_block != pages_per_block - 1)
            def _():
                ksum[...] = ksum[...] + colsum

            @pl.when(pk % pages_per_block == 0)
            def _():
                km_scr[pl.ds(pk // pages_per_block, 1), :] = ksum[...] * (1.0 / MOBA_BLOCK)
        s_scr[:, pl.ds(pl.multiple_of(pk * PAGE_SIZE, PAGE_SIZE), PAGE_SIZE)] = s

    @pl.when(jnp.logical_and(phase == 1, p == 0))
    def _():
        chunk = 2048
        n_chunks = past // chunk
        lane_c = lax.broadcasted_iota(jnp.int32, (rows, chunk), 1)
        lane_p = lax.broadcasted_iota(jnp.int32, (rows, PAGE_SIZE), 1)
        own = s_scr[:, past:past + PAGE_SIZE]
        own_ok = jnp.logical_and(lane_p <= tok_col, lane_p < t_new)
        if fox:
            lane_h = lax.broadcasted_iota(jnp.int32, (N_HEADS, N_HEADS), 1)
            sub_h = lax.broadcasted_iota(jnp.int32, (N_HEADS, N_HEADS), 0)
            cum = []
            for t in range(t_new):
                lrow = jnp.broadcast_to(lfn_ref[t:t + 1, :], (N_HEADS, N_HEADS))
                col = jnp.sum(jnp.where(lane_h == sub_h, lrow, 0.0), axis=-1, keepdims=True)
                cum.append(col if t == 0 else cum[-1] + col)
            c_new = jnp.concatenate(cum, axis=0)
            c_key = jnp.zeros((rows, PAGE_SIZE), F32)
            for t in range(t_new):
                c_key = jnp.where(lane_p == t, jnp.concatenate([cum[t]] * t_new, axis=0), c_key)
            own = jnp.where(own_ok, own + (c_new - c_key), NEG)
        else:
            slope = jnp.zeros((rows, 1), F32)
            for h in range(N_HEADS):
                slope = jnp.where(head_col == h, _SLOPES[h], slope)
            t_abs = (past + tok_col).astype(F32)
            n_full = past // MOBA_BLOCK
            q_hi, q_lo = _split(qx[...], 2)
            k_hi, k_lo = _split(km_scr[...], 2)
            gate = _dot_nt(q_hi, k_hi) + (_dot_nt(q_hi, k_lo) + _dot_nt(q_lo, k_hi))
            gate = jnp.where(lane_p < n_full, gate, -jnp.inf)
            sel = _top_k_mask(gate, MOBA_TOPK, n_full)
            sel_bias = jnp.where(sel, 0.0, NEG).astype(BF16)
            own = jnp.where(own_ok, own - slope * (tok_col - lane_p).astype(F32), NEG)
        s_scr[:, past:past + PAGE_SIZE] = own
        m0 = jnp.max(own, axis=-1, keepdims=True)

        def bias_pass(ci, m):
            off = pl.multiple_of(ci * chunk, chunk)
            s = s_scr[:, pl.ds(off, chunk)]
            if fox:
                s = s + c_new
            else:
                blk = lax.broadcasted_iota(jnp.int32, (LANES, chunk), 0)
                key_blk = (lax.broadcasted_iota(jnp.int32, (LANES, chunk), 1) + ci * chunk) // MOBA_BLOCK
                expand = jnp.where(blk == key_blk, 1.0, 0.0).astype(BF16)
                s = s + _dot(sel_bias, expand)
                s = s - slope * (t_abs - (lane_c + ci * chunk).astype(F32))
            s_scr[:, pl.ds(off, chunk)] = s
            return jnp.maximum(m, jnp.max(s, axis=-1, keepdims=True))

        m = lax.fori_loop(0, n_chunks, bias_pass, m0)

        def exp_pass(ci, l):
            off = pl.multiple_of(ci * chunk, chunk)
            pexp = jnp.exp(s_scr[:, pl.ds(off, chunk)] - m)
            s_scr[:, pl.ds(off, chunk)] = pexp
            return l + jnp.sum(pexp, axis=-1, keepdims=True)

        p_own = jnp.exp(own - m)
        l = lax.fori_loop(0, n_chunks, exp_pass, jnp.sum(p_own, axis=-1, keepdims=True))
        l_scr[...] = l
        acc[...] = _dot(p_own.astype(BF16), vpad[...])

    @pl.when(phase == 1)
    def _():
        prob = s_scr[:, pl.ds(pl.multiple_of(p * PAGE_SIZE, PAGE_SIZE), PAGE_SIZE)]
        acc[...] = acc[...] + _dot(prob.astype(BF16), cv_ref[...].astype(BF16))

    @pl.when(jnp.logical_and(phase == 1, p == n_pages - 1))
    def _():
        head_mask = _head_of_lane((rows, D_MODEL), 1) == lax.broadcasted_iota(jnp.int32, (rows, D_MODEL), 0) % N_HEADS
        out = jnp.where(head_mask, acc[...] / l_scr[...], 0.0)
        for t in range(t_new):
            o_ref[t:t + 1, :] = jnp.sum(out[t * N_HEADS:(t + 1) * N_HEADS, :], axis=0, keepdims=True)


def _sample_attn(q, k_new, v_new, cache_k, cache_v, layer, page_table, cache_lf=None, lf_new=None):
    n_seq, t_new, _ = q.shape
    n_pages = page_table.shape[1]
    fox = cache_lf is not None
    rows = t_new * N_HEADS
    past = n_pages * PAGE_SIZE
    seq_blk = lambda s, ph, p, pt: (s, 0, 0)

    def k_map(s, ph, p, pt):
        return (layer, pt[s * n_pages + jnp.where(ph == 0, n_pages - 1 - p, 0)], 0, 0)

    def v_map(s, ph, p, pt):
        return (layer, pt[s * n_pages + jnp.where(ph == 0, 0, p)], 0, 0)

    in_specs = [pl.BlockSpec((None, t_new, D_MODEL), seq_blk)] * 3 + [
        pl.BlockSpec((None, None, PAGE_SIZE, D_MODEL), k_map),
        pl.BlockSpec((None, None, PAGE_SIZE, D_MODEL), v_map)]
    args = [q, k_new, v_new, cache_k, cache_v]
    scratch = [pltpu.VMEM((rows, D_MODEL), F32),
               pltpu.VMEM((PAGE_SIZE, D_MODEL), BF16),
               pltpu.VMEM((PAGE_SIZE, D_MODEL), BF16),
               pltpu.VMEM((rows, past + PAGE_SIZE), F32),
               pltpu.VMEM((rows, D_MODEL), F32),
               pltpu.VMEM((rows, 1), F32)]
    if fox:
        in_specs += [pl.BlockSpec((None, None, PAGE_SIZE, N_HEADS), k_map),
                     pl.BlockSpec((None, t_new, N_HEADS), seq_blk)]
        args += [cache_lf, lf_new]
        scratch += [pltpu.VMEM((N_HEADS, 1), F32), pltpu.VMEM((PAGE_SIZE, LANES), F32)]
    else:
        scratch += [pltpu.VMEM((1, D_MODEL), F32), pltpu.VMEM((LANES, D_MODEL), F32)]
    return pl.pallas_call(
        functools.partial(_sample_attn_body, fox=fox, t_new=t_new, n_pages=n_pages),
        grid_spec=pltpu.PrefetchScalarGridSpec(
            num_scalar_prefetch=1,
            grid=(n_seq, 2, n_pages),
            in_specs=in_specs,
            out_specs=pl.BlockSpec((None, t_new, D_MODEL), seq_blk),
            scratch_shapes=scratch),
        out_shape=jax.ShapeDtypeStruct((n_seq, t_new, D_MODEL), F32),
        compiler_params=_params(("parallel", "arbitrary", "arbitrary")),
        name="fox_sample_attn" if fox else "moba_sample_attn",
    )(page_table.reshape(-1), *args)


def kernel(x_prompt, x_sample, cache_k_moba, cache_v_moba, cache_k_fox, cache_v_fox, cache_logf_fox, state_conv, page_table, attn_norm, ffn_norm, moba_w_qkv, moba_q_gain, moba_k_gain, moba_w_o, fox_w_qkvf, fox_b_f, fox_q_gain, fox_k_gain, fox_w_o, ffn_w_gate, ffn_w_up, ffn_conv_w, ffn_conv_b, ffn_w_down):
    b, n, d = x_prompt.shape
    db, t_new, _ = x_sample.shape
    depth = attn_norm.shape[0]
    f = ffn_w_gate.shape[-1]
    kv_shape = lambda a: a.reshape(a.shape[0], a.shape[1], PAGE_SIZE, D_MODEL)
    cache_k_moba, cache_v_moba = kv_shape(cache_k_moba), kv_shape(cache_v_moba)
    cache_k_fox, cache_v_fox = kv_shape(cache_k_fox), kv_shape(cache_v_fox)
    heads = lambda a, lead: a.reshape(lead + (N_HEADS, HEAD_DIM))

    yp = x_prompt.reshape(b * n, d)
    ys = x_sample.reshape(db * t_new, d)
    moba_out = [[] for _ in range(4)]
    fox_out = [[] for _ in range(6)]
    conv_p, conv_s = [], []
    zero_hist = jnp.zeros((b, CONV_WIDTH - 1, f), F32)
    for layer in range(depth):
        j = layer // 2
        if layer % 2 == 0:
            w = moba_w_qkv[j].astype(BF16)
            qp, kp, vp, kmean = _qkv_proj(yp, attn_norm[layer], w, moba_q_gain[j], moba_k_gain[j], kmean=True)
            qs, ks, vs = _qkv_proj(ys, attn_norm[layer], w, moba_q_gain[j], moba_k_gain[j])
            mix_p = _prompt_attn(qp, kp, vp, batch=b, kmean=kmean.reshape(b, n // MOBA_BLOCK, d))
            mix_s = _sample_attn(qs.reshape(db, t_new, d), ks.reshape(db, t_new, d), vs.reshape(db, t_new, d),
                                 cache_k_moba, cache_v_moba, j, page_table)
            w_o = moba_w_o[j]
            for lst, a in zip(moba_out, (heads(kp, (b, n)), heads(vp, (b, n)),
                                         heads(ks, (db, t_new)), heads(vs, (db, t_new)))):
                lst.append(a)
        else:
            w = fox_w_qkvf[j, :, :3 * d].astype(BF16)
            wf = jnp.pad(fox_w_qkvf[j, :, 3 * d:], ((0, 0), (0, LANES - N_HEADS))).astype(BF16)
            bf = jnp.pad(fox_b_f[j], (0, LANES - N_HEADS)).reshape(1, LANES)
            qp, kp, vp, lp = _qkv_proj(yp, attn_norm[layer], w, fox_q_gain[j], fox_k_gain[j], wf, bf)
            qs, ks, vs, ls = _qkv_proj(ys, attn_norm[layer], w, fox_q_gain[j], fox_k_gain[j], wf, bf)
            mix_p = _prompt_attn(qp, kp, vp, batch=b, c=_seq_cumsum(lp, n))
            mix_s = _sample_attn(qs.reshape(db, t_new, d), ks.reshape(db, t_new, d), vs.reshape(db, t_new, d),
                                 cache_k_fox, cache_v_fox, j, page_table,
                                 cache_lf=cache_logf_fox, lf_new=ls.reshape(db, t_new, N_HEADS))
            w_o = fox_w_o[j]
            for lst, a in zip(fox_out, (heads(kp, (b, n)), heads(vp, (b, n)), lp.reshape(b, n, N_HEADS),
                                        heads(ks, (db, t_new)), heads(vs, (db, t_new)),
                                        ls.reshape(db, t_new, N_HEADS))):
                lst.append(a)
        w_o = w_o.astype(BF16)
        yp = _wo_proj(yp, mix_p, w_o)
        ys = _wo_proj(ys, mix_s.reshape(db * t_new, d), w_o)
        ffn_w = (ffn_norm[layer], ffn_w_gate[layer].astype(BF16), ffn_w_up[layer].astype(BF16),
                 ffn_conv_w[layer], ffn_conv_b[layer], ffn_w_down[layer].astype(BF16))
        yp, cp = _conv_ffn(yp, zero_hist, *ffn_w, rows_per_seq=n)
        ys, cs = _conv_ffn(ys, state_conv[layer], *ffn_w, rows_per_seq=t_new)
        conv_p.append(cp)
        conv_s.append(cs)
    return (yp.reshape(b, n, d), ys.reshape(db, t_new, d),
            *[jnp.stack(a) for a in moba_out],
            *[jnp.stack(a) for a in fox_out],
            jnp.stack(conv_p), jnp.stack(conv_s))
```

```python
import functools

import numpy as np
import jax
import jax.numpy as jnp
from jax import lax
from jax.experimental import pallas as pl
from jax.experimental.pallas import tpu as pltpu

N_HEADS = 16
HEAD_DIM = 64
D_MODEL = N_HEADS * HEAD_DIM
MOBA_BLOCK = 256
MOBA_TOPK = 3
PAGE_SIZE = 128
CONV_WIDTH = 3
RMS_EPS = 1e-6
LANES = 128
SUBLANES = 8
HEADS_PER_TILE = LANES // HEAD_DIM
N_HEAD_PAIRS = N_HEADS // HEADS_PER_TILE
VMEM_LIMIT = 56 * 1024 * 1024
NEG = -1e30
ATTN_TILE = MOBA_BLOCK
Q_CHUNK = 128
PAGES_PER_STEP = 8
SPLIT_F32 = 3

F32 = jnp.float32
BF16 = jnp.bfloat16

_SLOPES = [float(s) for s in np.exp2(np.float32(-8.0) * np.arange(1, N_HEADS + 1, dtype=np.float32)
                                     / np.float32(N_HEADS)).astype(np.float32)]


def _dot(a, b):
    return jnp.dot(a, b, preferred_element_type=F32)


def _dot_nt(a, b):
    return lax.dot_general(a, b, (((1,), (1,)), ((), ())), preferred_element_type=F32)


def _split(a, terms):
    parts = []
    r = a
    for _ in range(terms):
        p = r.astype(BF16)
        parts.append(p)
        r = r - p.astype(F32)
    return parts


def _split_f32(a, terms):
    return [p.astype(F32) for p in _split(a, terms)]


def _dot_split_lhs(a, b_bf16, terms):
    out = None
    for p in _split(a, terms):
        d = _dot(p, b_bf16)
        out = d if out is None else out + d
    return out


def _dot_split_rhs(a_bf16, b, terms):
    out = None
    for p in _split(b, terms):
        d = _dot(a_bf16, p)
        out = d if out is None else out + d
    return out


def _params(sem):
    return pltpu.CompilerParams(dimension_semantics=sem, vmem_limit_bytes=VMEM_LIMIT)


def _resident(shape):
    return pl.BlockSpec(shape, lambda *_: (0,) * len(shape), pipeline_mode=pl.Buffered(1))


def _head_of_lane(shape, dim):
    return lax.broadcasted_iota(jnp.int32, shape, dim) // HEAD_DIM


def _log_sigmoid(f):
    return -(jnp.maximum(-f, 0.0) + jnp.log1p(jnp.exp(-jnp.abs(f))))


def _seg_mats():
    lane_head = np.arange(D_MODEL) // HEAD_DIM
    seg = (lane_head[:, None] == np.arange(LANES)[None, :]).astype(np.float32)
    return jnp.asarray(seg, BF16), jnp.asarray(seg.T, BF16)


def _rms_rows(x, gain):
    ms = jnp.mean(x * x, axis=-1, keepdims=True)
    return ((x * lax.rsqrt(ms + RMS_EPS)) * gain).astype(BF16)


def _head_norm_rows(a, gain, seg, segt):
    ss = _dot_split_lhs(a * a, seg, 2)
    rs = lax.rsqrt(ss * (1.0 / HEAD_DIM) + RMS_EPS)
    return (a * _dot_split_lhs(rs, segt, 2)) * gain


def _qkv_rows_body(*refs, fox):
    x_ref, g_ref, w_ref, qg_ref, kg_ref, seg_ref, segt_ref = refs[:7]
    if fox:
        wf_ref, bf_ref, q_ref, k_ref, v_ref, lf_ref = refs[7:]
    else:
        q_ref, k_ref, v_ref = refs[7:]
    h = _rms_rows(x_ref[...], g_ref[...])
    qkv = _dot(h, w_ref[...])
    seg = seg_ref[...]
    segt = segt_ref[...]
    q_ref[...] = _head_norm_rows(qkv[:, :D_MODEL], qg_ref[...], seg, segt)
    k_ref[...] = _head_norm_rows(qkv[:, D_MODEL:2 * D_MODEL], kg_ref[...], seg, segt)
    v_ref[...] = qkv[:, 2 * D_MODEL:]
    if fox:
        lf_ref[...] = _log_sigmoid(_dot(h, wf_ref[...]) + bf_ref[...])[:, :N_HEADS]


def _qkv_rows(x, gain, w, q_gain, k_gain, wf=None, bf=None):
    n = x.shape[0]
    tm = min(MOBA_BLOCK, n)
    fox = wf is not None
    seg, segt = _seg_mats()
    row = lambda i: (i, 0)
    in_specs = [pl.BlockSpec((tm, D_MODEL), row), _resident((1, D_MODEL)), _resident((D_MODEL, 3 * D_MODEL)),
                _resident((1, D_MODEL)), _resident((1, D_MODEL)),
                _resident((D_MODEL, LANES)), _resident((LANES, D_MODEL))]
    args = [x, gain.reshape(1, D_MODEL), w,
            jnp.tile(q_gain, N_HEADS).reshape(1, D_MODEL), jnp.tile(k_gain, N_HEADS).reshape(1, D_MODEL),
            seg, segt]
    out_shape = [jax.ShapeDtypeStruct((n, D_MODEL), F32)] * 3
    out_specs = [pl.BlockSpec((tm, D_MODEL), row)] * 3
    if fox:
        in_specs += [_resident((D_MODEL, LANES)), _resident((1, LANES))]
        args += [wf, bf]
        out_shape.append(jax.ShapeDtypeStruct((n, N_HEADS), F32))
        out_specs.append(pl.BlockSpec((tm, N_HEADS), row))
    return pl.pallas_call(
        functools.partial(_qkv_rows_body, fox=fox),
        grid=(n // tm,),
        in_specs=in_specs, out_specs=out_specs, out_shape=out_shape,
        compiler_params=_params(("parallel",)),
        name="qkv_rows",
    )(*args)


def _qkv_prompt_body(*refs, fox):
    x_ref, g_ref, wq_ref, wkt_ref, wvt_ref, qg_ref, kg_ref, seg_ref, segt_ref = refs[:9]
    if fox:
        wft_ref, bf_ref, q_ref, kt_ref, ktb_ref, vt_ref, vtb_ref, lft_ref = refs[9:]
    else:
        q_ref, kt_ref, ktb_ref, vt_ref, vtb_ref, km_ref = refs[9:]
    h = _rms_rows(x_ref[...], g_ref[...])
    seg = seg_ref[...]
    segt = segt_ref[...]
    q_ref[...] = _head_norm_rows(_dot(h, wq_ref[...]), qg_ref[...], seg, segt)

    kt = _dot_nt(wkt_ref[...], h)
    ss = _dot_split_rhs(segt, kt * kt, 2)
    rs = lax.rsqrt(ss * (1.0 / HEAD_DIM) + RMS_EPS)
    kt = (kt * _dot_split_rhs(seg, rs, 2)) * kg_ref[...]
    kt_ref[...] = kt
    ktb_ref[...] = kt.astype(BF16)
    vt = _dot_nt(wvt_ref[...], h)
    vt_ref[...] = vt
    vtb_ref[...] = vt.astype(BF16)
    if fox:
        lft_ref[...] = _log_sigmoid(_dot_nt(wft_ref[...], h) + bf_ref[...])[:N_HEADS, :]
    else:
        ones = jnp.ones((SUBLANES, kt.shape[1]), BF16)
        total = None
        for part in _split(kt, SPLIT_F32):
            d = _dot_nt(ones, part)
            total = d if total is None else total + d
        km_ref[...] = total[0:1, :] * (1.0 / MOBA_BLOCK)


def _qkv_prompt(x, gain, w, q_gain, k_gain, *, batch, wf=None, bf=None):
    n = x.shape[0]
    seq = n // batch
    tm = MOBA_BLOCK
    ns = seq // tm
    fox = wf is not None
    seg, segt = _seg_mats()
    row = lambda b, i: (b * ns + i, 0)
    col = lambda b, i: (b, 0, i)
    wq = w[:, :D_MODEL]
    wkt = w[:, D_MODEL:2 * D_MODEL].T
    wvt = w[:, 2 * D_MODEL:].T
    in_specs = [pl.BlockSpec((tm, D_MODEL), row), _resident((1, D_MODEL)),
                _resident((D_MODEL, D_MODEL)), _resident((D_MODEL, D_MODEL)), _resident((D_MODEL, D_MODEL)),
                _resident((1, D_MODEL)), _resident((D_MODEL, 1)),
                _resident((D_MODEL, LANES)), _resident((LANES, D_MODEL))]
    args = [x, gain.reshape(1, D_MODEL), wq, wkt, wvt,
            jnp.tile(q_gain, N_HEADS).reshape(1, D_MODEL), jnp.tile(k_gain, N_HEADS).reshape(D_MODEL, 1),
            seg, segt]
    t_spec = pl.BlockSpec((None, D_MODEL, tm), col)
    t_f32 = jax.ShapeDtypeStruct((batch, D_MODEL, seq), F32)
    t_bf16 = jax.ShapeDtypeStruct((batch, D_MODEL, seq), BF16)
    out_shape = [jax.ShapeDtypeStruct((n, D_MODEL), F32), t_f32, t_bf16, t_f32, t_bf16]
    out_specs = [pl.BlockSpec((tm, D_MODEL), row), t_spec, t_spec, t_spec, t_spec]
    if fox:
        in_specs += [_resident((LANES, D_MODEL)), _resident((LANES, 1))]
        args += [wf.T, bf.reshape(LANES, 1)]
        out_shape.append(jax.ShapeDtypeStruct((batch, N_HEADS, seq), F32))
        out_specs.append(pl.BlockSpec((None, N_HEADS, tm), col))
    else:
        out_shape.append(jax.ShapeDtypeStruct((batch * ns, 1, D_MODEL), F32))
        out_specs.append(pl.BlockSpec((None, 1, D_MODEL), lambda b, i: (b * ns + i, 0, 0)))
    return pl.pallas_call(
        functools.partial(_qkv_prompt_body, fox=fox),
        grid=(batch, ns),
        in_specs=in_specs, out_specs=out_specs, out_shape=out_shape,
        compiler_params=_params(("parallel", "parallel")),
        name="qkv_prompt",
    )(*args)


def _wo_body(x_ref, a_ref, w_ref, o_ref):
    o_ref[...] = x_ref[...] + _dot(a_ref[...].astype(BF16), w_ref[...])


def _wo_proj(x, a, w):
    n = x.shape[0]
    tm = min(512, n)
    row = lambda i: (i, 0)
    return pl.pallas_call(
        _wo_body,
        grid=(n // tm,),
        in_specs=[pl.BlockSpec((tm, D_MODEL), row), pl.BlockSpec((tm, D_MODEL), row),
                  _resident((D_MODEL, D_MODEL))],
        out_specs=pl.BlockSpec((tm, D_MODEL), row),
        out_shape=jax.ShapeDtypeStruct((n, D_MODEL), F32),
        compiler_params=_params(("parallel",)),
        name="wo_proj",
    )(x, a, w)


def _ffn_body(*refs, tm, sample, tiles_per_seq):
    if sample:
        (x_ref, g_ref, wg_ref, wu_ref, cw_ref, cb_ref, wd_ref, h1_ref, h2_ref,
         y_ref, gout_ref, gbuf) = refs
    else:
        (x_ref, g_ref, wg_ref, wu_ref, cw_ref, cb_ref, wd_ref, hist_ref,
         y_ref, gout_ref, gbuf) = refs
    halo = SUBLANES
    x = x_ref[...]
    xn = _rms_rows(x, g_ref[...])
    g = _dot(xn, wg_ref[...])
    u = _dot(xn, wu_ref[...])
    if sample:
        gbuf[0:halo, :] = jnp.zeros((halo, g.shape[1]), F32)
    else:
        i = pl.program_id(0)

        @pl.when(i % tiles_per_seq != 0)
        def _():
            gbuf[0:halo, :] = gbuf[tm:tm + halo, :]

        @pl.when(i % tiles_per_seq == 0)
        def _():
            gbuf[halo - (CONV_WIDTH - 1):halo, :] = hist_ref[...]
    gbuf[halo:halo + tm, :] = g
    prev1 = gbuf[halo - 1:halo - 1 + tm, :]
    prev2 = gbuf[halo - 2:halo - 2 + tm, :]
    if sample:
        t = lax.broadcasted_iota(jnp.int32, (tm, 1), 0) % tiles_per_seq
        prev1 = jnp.where(t == 0, h1_ref[...], prev1)
        prev2 = jnp.where(t < 2, h2_ref[...], prev2)
        gout_ref[...] = g
    else:
        gout_ref[...] = g[tm - (CONV_WIDTH - 1):, :]
    cw = cw_ref[...]
    g_conv = cb_ref[...] + cw[0:1, :] * prev2
    g_conv = g_conv + cw[1:2, :] * prev1
    g_conv = g_conv + cw[2:3, :] * g
    hidden = (g_conv * jax.nn.sigmoid(g_conv)) * u
    y_ref[...] = x + _dot(hidden.astype(BF16), wd_ref[...])


def _conv_ffn(x, hist, gain, wg, wu, cw, cb, wd, *, rows_per_seq):
    n = x.shape[0]
    f = wg.shape[1]
    n_seq = n // rows_per_seq
    sample = rows_per_seq < SUBLANES
    row = lambda i: (i, 0)
    w_specs = [_resident((1, D_MODEL)), _resident((D_MODEL, f)), _resident((D_MODEL, f)),
               _resident((CONV_WIDTH, f)), _resident((1, f)), _resident((f, D_MODEL))]
    w_args = [gain.reshape(1, D_MODEL), wg, wu, cw, cb.reshape(1, f), wd]
    if sample:
        tm = n
        t = jnp.arange(n) % rows_per_seq
        h_rep = jnp.repeat(hist, rows_per_seq, axis=0)
        h1 = h_rep[:, 1]
        h2 = jnp.where((t == 0)[:, None], h_rep[:, 0], h_rep[:, 1])
        y, g = pl.pallas_call(
            functools.partial(_ffn_body, tm=tm, sample=True, tiles_per_seq=rows_per_seq),
            grid=(1,),
            in_specs=[pl.BlockSpec((tm, D_MODEL), row)] + w_specs
                     + [pl.BlockSpec((tm, f), row), pl.BlockSpec((tm, f), row)],
            out_specs=[pl.BlockSpec((tm, D_MODEL), row), pl.BlockSpec((tm, f), row)],
            out_shape=[jax.ShapeDtypeStruct((n, D_MODEL), F32), jax.ShapeDtypeStruct((n, f), F32)],
            scratch_shapes=[pltpu.VMEM((tm + SUBLANES, f), F32)],
            compiler_params=_params(("arbitrary",)),
            name="conv_ffn_sample",
        )(x, *w_args, h1, h2)
        return y, g.reshape(n_seq, rows_per_seq, f)[:, rows_per_seq - (CONV_WIDTH - 1):]
    tm = min(256, rows_per_seq)
    tiles_per_seq = rows_per_seq // tm
    seq_blk = lambda i: (i // tiles_per_seq, 0, 0)
    y, new_state = pl.pallas_call(
        functools.partial(_ffn_body, tm=tm, sample=False, tiles_per_seq=tiles_per_seq),
        grid=(n // tm,),
        in_specs=[pl.BlockSpec((tm, D_MODEL), row)] + w_specs
                 + [pl.BlockSpec((None, CONV_WIDTH - 1, f), seq_blk)],
        out_specs=[pl.BlockSpec((tm, D_MODEL), row), pl.BlockSpec((None, CONV_WIDTH - 1, f), seq_blk)],
        out_shape=[jax.ShapeDtypeStruct((n, D_MODEL), F32),
                   jax.ShapeDtypeStruct((n_seq, CONV_WIDTH - 1, f), F32)],
        scratch_shapes=[pltpu.VMEM((tm + SUBLANES, f), F32)],
        compiler_params=_params(("arbitrary",)),
        name="conv_ffn_prompt",
    )(x, *w_args, hist)
    return y, new_state


def _cumsum_body(lft_ref, ct_ref, carry, *, tm):
    @pl.when(pl.program_id(1) == 0)
    def _():
        carry[...] = jnp.zeros_like(carry)

    r = lax.broadcasted_iota(jnp.int32, (tm, tm), 0)
    c = lax.broadcasted_iota(jnp.int32, (tm, tm), 1)
    upto = jnp.where(r <= c, 1.0, 0.0).astype(BF16)
    out = _dot_split_lhs(lft_ref[...], upto, SPLIT_F32) + carry[...]
    ct_ref[...] = out
    carry[...] = out[:, tm - 1:tm]


def _seq_cumsum(lft):
    batch, _, seq = lft.shape
    tm = min(512, seq)
    spec = pl.BlockSpec((None, N_HEADS, tm), lambda b, i: (b, 0, i))
    return pl.pallas_call(
        functools.partial(_cumsum_body, tm=tm),
        grid=(batch, seq // tm),
        in_specs=[spec], out_specs=spec,
        out_shape=jax.ShapeDtypeStruct(lft.shape, F32),
        scratch_shapes=[pltpu.VMEM((N_HEADS, 1), F32)],
        compiler_params=_params(("parallel", "arbitrary")),
        name="logf_cumsum",
    )(lft)


def _top_k_rows(gate, k, n_valid):
    idx_of = lax.broadcasted_iota(jnp.int32, gate.shape, 0)
    sel = jnp.zeros(gate.shape, jnp.bool_)
    for r in range(k):
        mx = jnp.max(gate, axis=0, keepdims=True)
        idx = jnp.min(jnp.where(gate == mx, idx_of, gate.shape[0]), axis=0, keepdims=True)
        hit = idx_of == idx
        sel = jnp.logical_or(sel, jnp.logical_and(hit, r < n_valid))
        gate = jnp.where(hit, -jnp.inf, gate)
    return sel


def _stack_rows(blocks, single_rows, width):
    used = sum(p.shape[0] for p in blocks)
    out = jnp.concatenate(blocks + [jnp.zeros((LANES - used, width), F32)], axis=0)
    row = lax.broadcasted_iota(jnp.int32, (LANES, width), 0)
    for k, r in enumerate(single_rows):
        out = jnp.where(row == used + k, r, out)
    return out


_MOBA_FEATURE_ROWS = 16


def _prompt_attn_body(*refs, fox):
    qi_ref, kn_ref = refs[:2]
    if fox:
        q_ref, kt_ref, vt_ref, ctq_ref, ctk_ref, o_ref, qa, m_scr, l_scr, acc = refs[2:]
    else:
        q_ref, kt_ref, vt_ref, km_ref, o_ref, qa, m_scr, l_scr, acc = refs[2:]
    t = ATTN_TILE
    step = pl.program_id(1)
    i = qi_ref[step]
    n = kn_ref[step]
    scale = HEAD_DIM ** -0.5
    lane = lax.broadcasted_iota(jnp.int32, (t, LANES), 1)
    lane_head = lane // HEAD_DIM
    one_row = jnp.ones((1, t), F32)

    @pl.when(n == 0)
    def _():
        m_scr[...] = jnp.full(m_scr.shape, NEG, F32)
        l_scr[...] = jnp.zeros(l_scr.shape, F32)
        acc[...] = jnp.zeros(acc.shape, F32)
        if fox:
            feat = _stack_rows(_split_f32(ctq_ref[...], SPLIT_F32)
                               + [jnp.ones((SPLIT_F32 * N_HEADS, t), F32)], [], t).T
        else:
            t_in_tile = lax.broadcasted_iota(jnp.int32, (1, t), 1).astype(F32)
            blk_of_row = lax.broadcasted_iota(jnp.int32, (_MOBA_FEATURE_ROWS, t), 0)
        for h in range(N_HEADS):
            hp, e = divmod(h, HEADS_PER_TILE)
            q2 = q_ref[:, hp * LANES:(hp + 1) * LANES]
            qh = jnp.where(lane_head == e, q2, 0.0)
            if fox:
                mine = jnp.logical_and(lane % N_HEADS == h, lane < 2 * SPLIT_F32 * N_HEADS)
                feat_h = jnp.where(mine, feat, 0.0)
            else:
                km2 = km_ref[:, hp * LANES:(hp + 1) * LANES]
                q_hi, q_lo = _split(qh, 2)
                k_hi, k_lo = _split(km2, 2)
                gate = _dot_nt(k_hi, q_hi) + (_dot_nt(k_hi, q_lo) + _dot_nt(k_lo, q_hi))
                gate = jnp.where(blk_of_row < i, gate, -jnp.inf)
                sel = _top_k_rows(gate, MOBA_TOPK, i)
                sel_bias = jnp.where(jnp.logical_or(sel, blk_of_row >= i), 0.0, NEG)
                slope = _SLOPES[h]
                feat_h = _stack_rows(
                    [sel_bias],
                    _split_f32(-slope * t_in_tile, SPLIT_F32)
                    + _split_f32(slope * one_row, SPLIT_F32)
                    + _split_f32((-slope * t) * one_row, SPLIT_F32),
                    t).T
            qa[h] = jnp.concatenate([(qh * scale).astype(BF16), feat_h.astype(BF16)], axis=-1)

    row = lax.broadcasted_iota(jnp.int32, (Q_CHUNK, t), 0)
    col = lax.broadcasted_iota(jnp.int32, (Q_CHUNK, t), 1)
    allow = [(row - col + ((i - n) * t + c * Q_CHUNK)) >= 0 for c in range(t // Q_CHUNK)]
    if fox:
        neg_cs = [-p for p in _split_f32(ctk_ref[...], SPLIT_F32)]
        k_feat = _stack_rows([jnp.ones((SPLIT_F32 * N_HEADS, t), F32)] + neg_cs, [], t).astype(BF16)
    else:
        key_in_tile = lax.broadcasted_iota(jnp.int32, (1, t), 1).astype(F32)
        onehot = jnp.where(lax.broadcasted_iota(jnp.int32, (_MOBA_FEATURE_ROWS, t), 0) == n, 1.0, 0.0)
        k_feat = _stack_rows([onehot], [one_row] * SPLIT_F32 + [key_in_tile] * SPLIT_F32
                             + [one_row * (i - n).astype(F32)] * SPLIT_F32, t).astype(BF16)

    for hp in range(N_HEAD_PAIRS):
        sl = slice(hp * LANES, (hp + 1) * LANES)
        k2 = jnp.concatenate([kt_ref[sl, :], k_feat], axis=0)
        v2 = vt_ref[sl, :]
        alphas, pvs = [], []
        for e in range(HEADS_PER_TILE):
            h = hp * HEADS_PER_TILE + e
            alpha_c, pv_c = [], []
            for c in range(t // Q_CHUNK):
                rows = slice(c * Q_CHUNK, (c + 1) * Q_CHUNK)
                s = jnp.where(allow[c], _dot(qa[h, rows, :], k2), NEG)
                m_prev = m_scr[h, rows, :]
                m_new = jnp.maximum(m_prev, jnp.broadcast_to(jnp.max(s, axis=-1, keepdims=True), m_prev.shape))
                alpha = jnp.exp(m_prev - m_new)
                p = jnp.concatenate([jnp.exp(s[:, j * LANES:(j + 1) * LANES] - m_new)
                                     for j in range(t // LANES)], axis=-1)
                l_scr[h, rows, :] = (alpha * l_scr[h, rows, :]
                                     + jnp.broadcast_to(jnp.sum(p, axis=-1, keepdims=True), m_prev.shape))
                m_scr[h, rows, :] = m_new
                alpha_c.append(alpha)
                pv_c.append(_dot_nt(p.astype(BF16), v2))
            alphas.append(alpha_c)
            pvs.append(pv_c)
        first = _head_of_lane((Q_CHUNK, LANES), 1) == 0
        for c in range(t // Q_CHUNK):
            rows = slice(c * Q_CHUNK, (c + 1) * Q_CHUNK)
            acc[rows, sl] = (acc[rows, sl] * jnp.where(first, alphas[0][c], alphas[1][c])
                             + jnp.where(first, pvs[0][c], pvs[1][c]))

    @pl.when(n == i)
    def _():
        for hp in range(N_HEAD_PAIRS):
            sl = slice(hp * LANES, (hp + 1) * LANES)
            l2 = jnp.where(lane_head == 0, l_scr[hp * HEADS_PER_TILE], l_scr[hp * HEADS_PER_TILE + 1])
            o_ref[:, sl] = acc[:, sl] / l2


def _prompt_attn(q, ktb, vtb, *, kmean=None, ct=None):
    batch, _, seq = ktb.shape
    n = q.shape[0]
    t = ATTN_TILE
    nq = seq // t
    fox = ct is not None
    assert fox or nq <= _MOBA_FEATURE_ROWS
    pairs = [(i, j) for i in range(nq) for j in range(i + 1)]
    qi = jnp.asarray([p[0] for p in pairs], jnp.int32)
    kn = jnp.asarray([p[1] for p in pairs], jnp.int32)
    q_map = lambda b, s, qi, kn: (b * nq + qi[s], 0)
    k_map = lambda b, s, qi, kn: (b, 0, kn[s])
    in_specs = [pl.BlockSpec((t, D_MODEL), q_map), pl.BlockSpec((None, D_MODEL, t), k_map),
                pl.BlockSpec((None, D_MODEL, t), k_map)]
    args = [q, ktb, vtb]
    if fox:
        in_specs += [pl.BlockSpec((None, N_HEADS, t), lambda b, s, qi, kn: (b, 0, qi[s])),
                     pl.BlockSpec((None, N_HEADS, t), k_map)]
        args += [ct, ct]
    else:
        in_specs += [pl.BlockSpec((None, _MOBA_FEATURE_ROWS, D_MODEL), lambda b, s, qi, kn: (b, 0, 0))]
        pad = _MOBA_FEATURE_ROWS - kmean.shape[1]
        args += [jnp.pad(kmean, ((0, 0), (0, pad), (0, 0))) if pad else kmean]
    return pl.pallas_call(
        functools.partial(_prompt_attn_body, fox=fox),
        grid_spec=pltpu.PrefetchScalarGridSpec(
            num_scalar_prefetch=2,
            grid=(batch, len(pairs)),
            in_specs=in_specs,
            out_specs=pl.BlockSpec((t, D_MODEL), q_map),
            scratch_shapes=[pltpu.VMEM((N_HEADS, t, 2 * LANES), BF16),
                            pltpu.VMEM((N_HEADS, t, LANES), F32),
                            pltpu.VMEM((N_HEADS, t, LANES), F32),
                            pltpu.VMEM((t, D_MODEL), F32)]),
        out_shape=jax.ShapeDtypeStruct((n, D_MODEL), F32),
        compiler_params=_params(("parallel", "arbitrary")),
        name="fox_prompt_attn" if fox else "moba_prompt_attn",
    )(qi, kn, *args)


def _sample_attn_body(*refs, fox, t_new, n_pages):
    g_pages = PAGES_PER_STEP
    ck_refs = refs[4:4 + g_pages]
    cv_refs = refs[4 + g_pages:4 + 2 * g_pages]
    q_ref, kn_ref, vn_ref = refs[1:4]
    rest = refs[4 + 2 * g_pages:]
    if fox:
        clf_refs = rest[:g_pages]
        lfn_ref, o_ref, qx, kpad, vpad, s_scr, acc, l_scr, carry = rest[g_pages:]
    else:
        o_ref, qx, kpad, vpad, s_scr, acc, l_scr, kmt = rest
    rows = t_new * N_HEADS
    past = n_pages * PAGE_SIZE
    n_groups = n_pages // g_pages
    phase = pl.program_id(1)
    p = pl.program_id(2)
    scale = HEAD_DIM ** -0.5
    row_col = lax.broadcasted_iota(jnp.int32, (rows, 1), 0)
    tok_col = row_col // N_HEADS
    head_col = row_col % N_HEADS

    @pl.when(jnp.logical_and(phase == 0, p == 0))
    def _():
        head_mask = _head_of_lane((N_HEADS, D_MODEL), 1) == lax.broadcasted_iota(jnp.int32, (N_HEADS, D_MODEL), 0)
        for t in range(t_new):
            qrow = jnp.broadcast_to(q_ref[t:t + 1, :], (N_HEADS, D_MODEL))
            qx[t * N_HEADS:(t + 1) * N_HEADS, :] = jnp.where(head_mask, qrow * scale, 0.0)
        kpad[...] = jnp.zeros(kpad.shape, BF16)
        vpad[...] = jnp.zeros(vpad.shape, BF16)
        kpad[0:t_new, :] = kn_ref[...].astype(BF16)
        vpad[0:t_new, :] = vn_ref[...].astype(BF16)
        s_scr[:, past:past + PAGE_SIZE] = _dot_nt(qx[...].astype(BF16), kpad[...])
        if fox:
            carry[...] = jnp.zeros(carry.shape, F32)
        else:
            kmt[...] = jnp.zeros(kmt.shape, F32)

    @pl.when(phase == 0)
    def _():
        group = n_groups - 1 - p
        qb = qx[...].astype(BF16)
        if fox:
            r0 = lax.broadcasted_iota(jnp.int32, (PAGE_SIZE, PAGE_SIZE), 0)
            r1 = lax.broadcasted_iota(jnp.int32, (PAGE_SIZE, PAGE_SIZE), 1)
            later = jnp.where(r0 > r1, 1.0, 0.0).astype(BF16)
        else:
            pages_per_block = MOBA_BLOCK // PAGE_SIZE
            lane_blk = lax.broadcasted_iota(jnp.int32, (D_MODEL, LANES), 1)
        for g in reversed(range(g_pages)):
            pk = group * g_pages + g
            kpage = ck_refs[g][...]
            s = _dot(qb, kpage.astype(BF16))
            if fox:
                lft = clf_refs[g][...]
                suffix = _dot_split_lhs(lft, later, SPLIT_F32) + carry[...]
                carry[...] = carry[...] + jnp.sum(lft, axis=-1, keepdims=True)
                s = s + jnp.concatenate([suffix] * t_new, axis=0)
            elif g % pages_per_block == 0:
                block_sum = kpage
                for g2 in range(g + 1, g + pages_per_block):
                    block_sum = block_sum + ck_refs[g2][...]
                mean = jnp.sum(block_sum, axis=-1, keepdims=True) * (1.0 / MOBA_BLOCK)
                kmt[...] = jnp.where(lane_blk == pk // pages_per_block, mean, kmt[...])
            s_scr[:, pl.ds(pl.multiple_of(pk * PAGE_SIZE, PAGE_SIZE), PAGE_SIZE)] = s

    @pl.when(jnp.logical_and(phase == 1, p == 0))
    def _():
        chunk = 2048 if past % 2048 == 0 else PAGE_SIZE
        n_chunks = past // chunk
        lane_c = lax.broadcasted_iota(jnp.int32, (rows, chunk), 1)
        lane_p = lax.broadcasted_iota(jnp.int32, (rows, PAGE_SIZE), 1)
        own = s_scr[:, past:past + PAGE_SIZE]
        own_ok = jnp.logical_and(lane_p <= tok_col, lane_p < t_new)
        if fox:
            lane_h = lax.broadcasted_iota(jnp.int32, (N_HEADS, N_HEADS), 1)
            sub_h = lax.broadcasted_iota(jnp.int32, (N_HEADS, N_HEADS), 0)
            cum = []
            for t in range(t_new):
                lrow = jnp.broadcast_to(lfn_ref[t:t + 1, :], (N_HEADS, N_HEADS))
                col = jnp.sum(jnp.where(lane_h == sub_h, lrow, 0.0), axis=-1, keepdims=True)
                cum.append(col if t == 0 else cum[-1] + col)
            c_new = jnp.concatenate(cum, axis=0)
            c_key = jnp.zeros((rows, PAGE_SIZE), F32)
            for t in range(t_new):
                c_key = jnp.where(lane_p == t, jnp.concatenate([cum[t]] * t_new, axis=0), c_key)
            own = jnp.where(own_ok, own + (c_new - c_key), NEG)
        else:
            slope = jnp.zeros((rows, 1), F32)
            for h in range(N_HEADS):
                slope = jnp.where(head_col == h, _SLOPES[h], slope)
            t_abs = (past + tok_col).astype(F32)
            n_full = past // MOBA_BLOCK
            q_hi, q_lo = _split(qx[...], 2)
            k_hi, k_lo = _split(kmt[...], 2)
            gate = _dot(q_hi, k_hi) + (_dot(q_hi, k_lo) + _dot(q_lo, k_hi))
            gate = jnp.where(lane_p < n_full, gate, -jnp.inf)
            sel = _top_k_lanes(gate, MOBA_TOPK, n_full)
            sel_bias = jnp.where(sel, 0.0, NEG).astype(BF16)
            own = jnp.where(own_ok, own - slope * (tok_col - lane_p).astype(F32), NEG)
        s_scr[:, past:past + PAGE_SIZE] = own
        m0 = jnp.max(own, axis=-1, keepdims=True)

        def bias_pass(ci, m):
            off = pl.multiple_of(ci * chunk, chunk)
            s = s_scr[:, pl.ds(off, chunk)]
            if fox:
                s = s + c_new
            else:
                blk = lax.broadcasted_iota(jnp.int32, (LANES, chunk), 0)
                key_blk = (lax.broadcasted_iota(jnp.int32, (LANES, chunk), 1) + ci * chunk) // MOBA_BLOCK
                expand = jnp.where(blk == key_blk, 1.0, 0.0).astype(BF16)
                s = s + _dot(sel_bias, expand)
                s = s - slope * (t_abs - (lane_c + ci * chunk).astype(F32))
            s_scr[:, pl.ds(off, chunk)] = s
            return jnp.maximum(m, jnp.max(s, axis=-1, keepdims=True))

        m = lax.fori_loop(0, n_chunks, bias_pass, m0)

        def exp_pass(ci, l):
            off = pl.multiple_of(ci * chunk, chunk)
            pexp = jnp.exp(s_scr[:, pl.ds(off, chunk)] - m)
            s_scr[:, pl.ds(off, chunk)] = pexp
            return l + jnp.sum(pexp, axis=-1, keepdims=True)

        p_own = jnp.exp(own - m)
        l = lax.fori_loop(0, n_chunks, exp_pass, jnp.sum(p_own, axis=-1, keepdims=True))
        l_scr[...] = l
        acc[...] = _dot(p_own.astype(BF16), vpad[...])

    @pl.when(phase == 1)
    def _():
        total = acc[...]
        for g in range(g_pages):
            off = pl.multiple_of((p * g_pages + g) * PAGE_SIZE, PAGE_SIZE)
            prob = s_scr[:, pl.ds(off, PAGE_SIZE)].astype(BF16)
            total = total + _dot_nt(prob, cv_refs[g][...].astype(BF16))
        acc[...] = total

    @pl.when(jnp.logical_and(phase == 1, p == n_groups - 1))
    def _():
        head_mask = _head_of_lane((rows, D_MODEL), 1) == lax.broadcasted_iota(jnp.int32, (rows, D_MODEL), 0) % N_HEADS
        out = jnp.where(head_mask, acc[...] / l_scr[...], 0.0)
        for t in range(t_new):
            o_ref[t:t + 1, :] = jnp.sum(out[t * N_HEADS:(t + 1) * N_HEADS, :], axis=0, keepdims=True)


def _top_k_lanes(gate, k, n_valid):
    lane = lax.broadcasted_iota(jnp.int32, gate.shape, 1)
    sel = jnp.zeros(gate.shape, jnp.bool_)
    for r in range(k):
        mx = jnp.max(gate, axis=-1, keepdims=True)
        idx = jnp.min(jnp.where(gate == mx, lane, LANES), axis=-1, keepdims=True)
        hit = lane == idx
        sel = jnp.logical_or(sel, jnp.logical_and(hit, r < n_valid))
        gate = jnp.where(hit, -jnp.inf, gate)
    return sel


def _sample_attn(q, k_new, v_new, cache_kt, cache_vt, layer, page_table, cache_lft=None, lf_new=None):
    n_seq, t_new, _ = q.shape
    n_pages = page_table.shape[1]
    g_pages = PAGES_PER_STEP
    assert n_pages % g_pages == 0 and (n_pages * PAGE_SIZE) % MOBA_BLOCK == 0
    n_groups = n_pages // g_pages
    fox = cache_lft is not None
    rows = t_new * N_HEADS
    past = n_pages * PAGE_SIZE
    seq_blk = lambda s, ph, p, pt: (s, 0, 0)

    def k_map(g):
        return lambda s, ph, p, pt: (
            layer, pt[s * n_pages + jnp.where(ph == 0, n_groups - 1 - p, 0) * g_pages + g], 0, 0)

    def v_map(g):
        return lambda s, ph, p, pt: (layer, pt[s * n_pages + jnp.where(ph == 0, 0, p) * g_pages + g], 0, 0)

    in_specs = ([pl.BlockSpec((None, t_new, D_MODEL), seq_blk)] * 3
                + [pl.BlockSpec((None, None, D_MODEL, PAGE_SIZE), k_map(g)) for g in range(g_pages)]
                + [pl.BlockSpec((None, None, D_MODEL, PAGE_SIZE), v_map(g)) for g in range(g_pages)])
    args = [q, k_new, v_new] + [cache_kt] * g_pages + [cache_vt] * g_pages
    scratch = [pltpu.VMEM((rows, D_MODEL), F32),
               pltpu.VMEM((PAGE_SIZE, D_MODEL), BF16),
               pltpu.VMEM((PAGE_SIZE, D_MODEL), BF16),
               pltpu.VMEM((rows, past + PAGE_SIZE), F32),
               pltpu.VMEM((rows, D_MODEL), F32),
               pltpu.VMEM((rows, 1), F32)]
    if fox:
        in_specs += [pl.BlockSpec((None, None, N_HEADS, PAGE_SIZE), k_map(g)) for g in range(g_pages)]
        in_specs += [pl.BlockSpec((None, t_new, N_HEADS), seq_blk)]
        args += [cache_lft] * g_pages + [lf_new]
        scratch += [pltpu.VMEM((N_HEADS, 1), F32)]
    else:
        scratch += [pltpu.VMEM((D_MODEL, LANES), F32)]
    return pl.pallas_call(
        functools.partial(_sample_attn_body, fox=fox, t_new=t_new, n_pages=n_pages),
        grid_spec=pltpu.PrefetchScalarGridSpec(
            num_scalar_prefetch=1,
            grid=(n_seq, 2, n_groups),
            in_specs=in_specs,
            out_specs=pl.BlockSpec((None, t_new, D_MODEL), seq_blk),
            scratch_shapes=scratch),
        out_shape=jax.ShapeDtypeStruct((n_seq, t_new, D_MODEL), F32),
        compiler_params=_params(("parallel", "arbitrary", "arbitrary")),
        name="fox_sample_attn" if fox else "moba_sample_attn",
    )(page_table.reshape(-1), *args)


def kernel(x_prompt, x_sample, cache_k_moba, cache_v_moba, cache_k_fox, cache_v_fox, cache_logf_fox, state_conv, page_table, attn_norm, ffn_norm, moba_w_qkv, moba_q_gain, moba_k_gain, moba_w_o, fox_w_qkvf, fox_b_f, fox_q_gain, fox_k_gain, fox_w_o, ffn_w_gate, ffn_w_up, ffn_conv_w, ffn_conv_b, ffn_w_down):
    b, n, d = x_prompt.shape
    db, t_new, _ = x_sample.shape
    depth = attn_norm.shape[0]
    f = ffn_w_gate.shape[-1]
    page_t = lambda a: a.transpose(0, 1, 3, 4, 2).reshape(a.shape[0], a.shape[1], D_MODEL, PAGE_SIZE)
    cache_k_moba, cache_v_moba = page_t(cache_k_moba), page_t(cache_v_moba)
    cache_k_fox, cache_v_fox = page_t(cache_k_fox), page_t(cache_v_fox)
    cache_lft = cache_logf_fox.transpose(0, 1, 3, 2)
    heads = lambda a, lead: a.reshape(lead + (N_HEADS, HEAD_DIM))
    heads_t = lambda a: a.reshape(b, N_HEADS, HEAD_DIM, n).transpose(0, 3, 1, 2)
    seq3 = lambda a: a.reshape(db, t_new, a.shape[-1])

    yp = x_prompt.reshape(b * n, d)
    ys = x_sample.reshape(db * t_new, d)
    moba_out = [[] for _ in range(4)]
    fox_out = [[] for _ in range(6)]
    conv_p, conv_s = [], []
    zero_hist = jnp.zeros((b, CONV_WIDTH - 1, f), F32)
    for layer in range(depth):
        j = layer // 2
        if layer % 2 == 0:
            w = moba_w_qkv[j].astype(BF16)
            qp, kt, ktb, vt, vtb, kmean = _qkv_prompt(yp, attn_norm[layer], w, moba_q_gain[j], moba_k_gain[j], batch=b)
            qs, ks, vs = _qkv_rows(ys, attn_norm[layer], w, moba_q_gain[j], moba_k_gain[j])
            mix_p = _prompt_attn(qp, ktb, vtb, kmean=kmean.reshape(b, n // MOBA_BLOCK, d))
            mix_s = _sample_attn(seq3(qs), seq3(ks), seq3(vs), cache_k_moba, cache_v_moba, j, page_table)
            w_o = moba_w_o[j]
            for lst, a in zip(moba_out, (heads_t(kt), heads_t(vt), heads(ks, (db, t_new)), heads(vs, (db, t_new)))):
                lst.append(a)
        else:
            w = fox_w_qkvf[j, :, :3 * d].astype(BF16)
            wf = jnp.pad(fox_w_qkvf[j, :, 3 * d:], ((0, 0), (0, LANES - N_HEADS))).astype(BF16)
            bf = jnp.pad(fox_b_f[j], (0, LANES - N_HEADS)).reshape(1, LANES)
            qp, kt, ktb, vt, vtb, lft = _qkv_prompt(yp, attn_norm[layer], w, fox_q_gain[j], fox_k_gain[j],
                                                    batch=b, wf=wf, bf=bf)
            qs, ks, vs, ls = _qkv_rows(ys, attn_norm[layer], w, fox_q_gain[j], fox_k_gain[j], wf, bf)
            mix_p = _prompt_attn(qp, ktb, vtb, ct=_seq_cumsum(lft))
            mix_s = _sample_attn(seq3(qs), seq3(ks), seq3(vs), cache_k_fox, cache_v_fox, j, page_table,
                                 cache_lft=cache_lft, lf_new=seq3(ls))
            w_o = fox_w_o[j]
            for lst, a in zip(fox_out, (heads_t(kt), heads_t(vt), lft.transpose(0, 2, 1),
                                        heads(ks, (db, t_new)), heads(vs, (db, t_new)), seq3(ls))):
                lst.append(a)
        w_o = w_o.astype(BF16)
        yp = _wo_proj(yp, mix_p, w_o)
        ys = _wo_proj(ys, mix_s.reshape(db * t_new, d), w_o)
        ffn_w = (ffn_norm[layer], ffn_w_gate[layer].astype(BF16), ffn_w_up[layer].astype(BF16),
                 ffn_conv_w[layer], ffn_conv_b[layer], ffn_w_down[layer].astype(BF16))
        yp, cp = _conv_ffn(yp, zero_hist, *ffn_w, rows_per_seq=n)
        ys, cs = _conv_ffn(ys, state_conv[layer], *ffn_w, rows_per_seq=t_new)
        conv_p.append(cp)
        conv_s.append(cs)
    return (yp.reshape(b, n, d), ys.reshape(db, t_new, d),
            *[jnp.stack(a) for a in moba_out],
            *[jnp.stack(a) for a in fox_out],
            jnp.stack(conv_p), jnp.stack(conv_s))
```

```python
import functools

import numpy as np
import jax
import jax.numpy as jnp
from jax import lax
from jax.experimental import pallas as pl
from jax.experimental.pallas import tpu as pltpu

N_HEADS = 16
HEAD_DIM = 64
D_MODEL = N_HEADS * HEAD_DIM
MOBA_BLOCK = 256
MOBA_TOPK = 3
PAGE_SIZE = 128
CONV_WIDTH = 3
RMS_EPS = 1e-6
LANES = 128
SUBLANES = 8
HEADS_PER_TILE = LANES // HEAD_DIM
N_HEAD_PAIRS = N_HEADS // HEADS_PER_TILE
VMEM_LIMIT = 56 * 1024 * 1024
NEG = -1e30
ATTN_TILE = MOBA_BLOCK
Q_CHUNK = 128
PAGES_PER_STEP = 16
SPLIT_F32 = 3
LOG2E = 1.4426950408889634

F32 = jnp.float32
BF16 = jnp.bfloat16

_SLOPES = [float(s) for s in np.exp2(np.float32(-8.0) * np.arange(1, N_HEADS + 1, dtype=np.float32)
                                     / np.float32(N_HEADS)).astype(np.float32)]


def _dot(a, b):
    return jnp.dot(a, b, preferred_element_type=F32)


def _dot_nt(a, b):
    return lax.dot_general(a, b, (((1,), (1,)), ((), ())), preferred_element_type=F32)


def _split(a, terms):
    parts = []
    r = a
    for _ in range(terms):
        p = r.astype(BF16)
        parts.append(p)
        r = r - p.astype(F32)
    return parts


def _split_f32(a, terms):
    return [p.astype(F32) for p in _split(a, terms)]


def _dot_split_lhs(a, b_bf16, terms):
    out = None
    for p in _split(a, terms):
        d = _dot(p, b_bf16)
        out = d if out is None else out + d
    return out


def _dot_split_rhs(a_bf16, b, terms):
    out = None
    for p in _split(b, terms):
        d = _dot(a_bf16, p)
        out = d if out is None else out + d
    return out


def _params(sem):
    return pltpu.CompilerParams(dimension_semantics=sem, vmem_limit_bytes=VMEM_LIMIT)


def _resident(shape):
    return pl.BlockSpec(shape, lambda *_: (0,) * len(shape), pipeline_mode=pl.Buffered(1))


def _head_of_lane(shape, dim):
    return lax.broadcasted_iota(jnp.int32, shape, dim) // HEAD_DIM


def _log_sigmoid(f):
    return -(jnp.maximum(-f, 0.0) + jnp.log1p(jnp.exp(-jnp.abs(f))))


def _seg_mats():
    lane_head = np.arange(D_MODEL) // HEAD_DIM
    seg = (lane_head[:, None] == np.arange(LANES)[None, :]).astype(np.float32)
    return jnp.asarray(seg, BF16), jnp.asarray(seg.T, BF16)


def _rms_rows(x, gain):
    ms = jnp.mean(x * x, axis=-1, keepdims=True)
    return ((x * lax.rsqrt(ms + RMS_EPS)) * gain).astype(BF16)


def _head_norm_rows(a, gain, seg, segt):
    ss = _dot_split_lhs(a * a, seg, 2)
    rs = lax.rsqrt(ss * (1.0 / HEAD_DIM) + RMS_EPS)
    return (a * _dot_split_lhs(rs, segt, 2)) * gain


def _qkv_rows_body(*refs, fox):
    x_ref, g_ref, w_ref, qg_ref, kg_ref, seg_ref, segt_ref = refs[:7]
    if fox:
        wf_ref, bf_ref, q_ref, k_ref, v_ref, lf_ref = refs[7:]
    else:
        q_ref, k_ref, v_ref = refs[7:]
    h = _rms_rows(x_ref[...], g_ref[...])
    qkv = _dot(h, w_ref[...])
    seg = seg_ref[...]
    segt = segt_ref[...]
    q_ref[...] = _head_norm_rows(qkv[:, :D_MODEL], qg_ref[...], seg, segt)
    k_ref[...] = _head_norm_rows(qkv[:, D_MODEL:2 * D_MODEL], kg_ref[...], seg, segt)
    v_ref[...] = qkv[:, 2 * D_MODEL:]
    if fox:
        lf_ref[...] = _log_sigmoid(_dot(h, wf_ref[...]) + bf_ref[...])[:, :N_HEADS]


def _qkv_rows(x, gain, w, q_gain, k_gain, wf=None, bf=None):
    n = x.shape[0]
    tm = min(MOBA_BLOCK, n)
    fox = wf is not None
    seg, segt = _seg_mats()
    row = lambda i: (i, 0)
    in_specs = [pl.BlockSpec((tm, D_MODEL), row), _resident((1, D_MODEL)), _resident((D_MODEL, 3 * D_MODEL)),
                _resident((1, D_MODEL)), _resident((1, D_MODEL)),
                _resident((D_MODEL, LANES)), _resident((LANES, D_MODEL))]
    args = [x, gain.reshape(1, D_MODEL), w,
            jnp.tile(q_gain, N_HEADS).reshape(1, D_MODEL), jnp.tile(k_gain, N_HEADS).reshape(1, D_MODEL),
            seg, segt]
    out_shape = [jax.ShapeDtypeStruct((n, D_MODEL), F32)] * 3
    out_specs = [pl.BlockSpec((tm, D_MODEL), row)] * 3
    if fox:
        in_specs += [_resident((D_MODEL, LANES)), _resident((1, LANES))]
        args += [wf, bf]
        out_shape.append(jax.ShapeDtypeStruct((n, N_HEADS), F32))
        out_specs.append(pl.BlockSpec((tm, N_HEADS), row))
    return pl.pallas_call(
        functools.partial(_qkv_rows_body, fox=fox),
        grid=(n // tm,),
        in_specs=in_specs, out_specs=out_specs, out_shape=out_shape,
        compiler_params=_params(("parallel",)),
        name="qkv_rows",
    )(*args)


def _head_norm_cols(at, gain_col):
    a3 = at.reshape(N_HEADS, HEAD_DIM, at.shape[1])
    ss = jnp.sum(a3 * a3, axis=1, keepdims=True)
    a3 = a3 * lax.rsqrt(ss * (1.0 / HEAD_DIM) + RMS_EPS)
    return a3.reshape(at.shape) * gain_col


def _qkv_prompt_body(*refs, fox):
    x_ref, g_ref, wqt_ref, wkt_ref, wvt_ref, qg_ref, kg_ref = refs[:7]
    if fox:
        wft_ref, bf_ref, q_ref, kt_ref, ktb_ref, vt_ref, vtb_ref, lft_ref = refs[7:]
    else:
        q_ref, kt_ref, ktb_ref, vt_ref, vtb_ref, km_ref = refs[7:]
    h = _rms_rows(x_ref[...], g_ref[...])
    q_ref[...] = _head_norm_cols(_dot_nt(wqt_ref[...], h), qg_ref[...]).T

    kt = _head_norm_cols(_dot_nt(wkt_ref[...], h), kg_ref[...])
    kt_ref[...] = kt
    ktb_ref[...] = kt.astype(BF16)
    vt = _dot_nt(wvt_ref[...], h)
    vt_ref[...] = vt
    vtb_ref[...] = vt.astype(BF16)
    if fox:
        lft_ref[...] = _log_sigmoid(_dot_nt(wft_ref[...], h) + bf_ref[...])[:N_HEADS, :]
    else:
        ones = jnp.ones((SUBLANES, kt.shape[1]), BF16)
        total = None
        for part in _split(kt, SPLIT_F32):
            d = _dot_nt(ones, part)
            total = d if total is None else total + d
        km_ref[...] = total[0:1, :] * (1.0 / MOBA_BLOCK)


def _qkv_prompt(x, gain, w, q_gain, k_gain, *, batch, wf=None, bf=None):
    n = x.shape[0]
    seq = n // batch
    tm = MOBA_BLOCK
    ns = seq // tm
    fox = wf is not None
    row = lambda b, i: (b * ns + i, 0)
    col = lambda b, i: (b, 0, i)
    wqt = w[:, :D_MODEL].T
    wkt = w[:, D_MODEL:2 * D_MODEL].T
    wvt = w[:, 2 * D_MODEL:].T
    in_specs = [pl.BlockSpec((tm, D_MODEL), row), _resident((1, D_MODEL)),
                _resident((D_MODEL, D_MODEL)), _resident((D_MODEL, D_MODEL)), _resident((D_MODEL, D_MODEL)),
                _resident((D_MODEL, 1)), _resident((D_MODEL, 1))]
    args = [x, gain.reshape(1, D_MODEL), wqt, wkt, wvt,
            jnp.tile(q_gain, N_HEADS).reshape(D_MODEL, 1), jnp.tile(k_gain, N_HEADS).reshape(D_MODEL, 1)]
    t_spec = pl.BlockSpec((None, D_MODEL, tm), col)
    t_f32 = jax.ShapeDtypeStruct((batch, D_MODEL, seq), F32)
    t_bf16 = jax.ShapeDtypeStruct((batch, D_MODEL, seq), BF16)
    out_shape = [jax.ShapeDtypeStruct((n, D_MODEL), F32), t_f32, t_bf16, t_f32, t_bf16]
    out_specs = [pl.BlockSpec((tm, D_MODEL), row), t_spec, t_spec, t_spec, t_spec]
    if fox:
        in_specs += [_resident((LANES, D_MODEL)), _resident((LANES, 1))]
        args += [wf.T, bf.reshape(LANES, 1)]
        out_shape.append(jax.ShapeDtypeStruct((batch, N_HEADS, seq), F32))
        out_specs.append(pl.BlockSpec((None, N_HEADS, tm), col))
    else:
        out_shape.append(jax.ShapeDtypeStruct((batch * ns, 1, D_MODEL), F32))
        out_specs.append(pl.BlockSpec((None, 1, D_MODEL), lambda b, i: (b * ns + i, 0, 0)))
    return pl.pallas_call(
        functools.partial(_qkv_prompt_body, fox=fox),
        grid=(batch, ns),
        in_specs=in_specs, out_specs=out_specs, out_shape=out_shape,
        compiler_params=_params(("parallel", "parallel")),
        name="qkv_prompt",
    )(*args)


def _wo_body(x_ref, a_ref, w_ref, o_ref):
    o_ref[...] = x_ref[...] + _dot(a_ref[...].astype(BF16), w_ref[...])


def _wo_proj(x, a, w):
    n = x.shape[0]
    tm = min(512, n)
    row = lambda i: (i, 0)
    return pl.pallas_call(
        _wo_body,
        grid=(n // tm,),
        in_specs=[pl.BlockSpec((tm, D_MODEL), row), pl.BlockSpec((tm, D_MODEL), row),
                  _resident((D_MODEL, D_MODEL))],
        out_specs=pl.BlockSpec((tm, D_MODEL), row),
        out_shape=jax.ShapeDtypeStruct((n, D_MODEL), F32),
        compiler_params=_params(("parallel",)),
        name="wo_proj",
    )(x, a, w)


def _ffn_body(*refs, tm, sample, tiles_per_seq):
    if sample:
        (x_ref, g_ref, wg_ref, wu_ref, cw_ref, cb_ref, wd_ref, h1_ref, h2_ref,
         y_ref, gout_ref, gbuf) = refs
    else:
        (x_ref, g_ref, wg_ref, wu_ref, cw_ref, cb_ref, wd_ref, hist_ref,
         y_ref, gout_ref, gbuf) = refs
    halo = SUBLANES
    x = x_ref[...]
    xn = _rms_rows(x, g_ref[...])
    g = _dot(xn, wg_ref[...])
    u = _dot(xn, wu_ref[...])
    if sample:
        gbuf[0:halo, :] = jnp.zeros((halo, g.shape[1]), F32)
    else:
        i = pl.program_id(0)

        @pl.when(i % tiles_per_seq != 0)
        def _():
            gbuf[0:halo, :] = gbuf[tm:tm + halo, :]

        @pl.when(i % tiles_per_seq == 0)
        def _():
            gbuf[halo - (CONV_WIDTH - 1):halo, :] = hist_ref[...]
    gbuf[halo:halo + tm, :] = g
    prev1 = gbuf[halo - 1:halo - 1 + tm, :]
    prev2 = gbuf[halo - 2:halo - 2 + tm, :]
    if sample:
        t = lax.broadcasted_iota(jnp.int32, (tm, 1), 0) % tiles_per_seq
        prev1 = jnp.where(t == 0, h1_ref[...], prev1)
        prev2 = jnp.where(t < 2, h2_ref[...], prev2)
        gout_ref[...] = g
    else:
        gout_ref[...] = g[tm - (CONV_WIDTH - 1):, :]
    cw = cw_ref[...]
    g_conv = cb_ref[...] + cw[0:1, :] * prev2
    g_conv = g_conv + cw[1:2, :] * prev1
    g_conv = g_conv + cw[2:3, :] * g
    hidden = (g_conv * jax.nn.sigmoid(g_conv)) * u
    y_ref[...] = x + _dot(hidden.astype(BF16), wd_ref[...])


def _conv_ffn(x, hist, gain, wg, wu, cw, cb, wd, *, rows_per_seq):
    n = x.shape[0]
    f = wg.shape[1]
    n_seq = n // rows_per_seq
    sample = rows_per_seq < SUBLANES
    row = lambda i: (i, 0)
    w_specs = [_resident((1, D_MODEL)), _resident((D_MODEL, f)), _resident((D_MODEL, f)),
               _resident((CONV_WIDTH, f)), _resident((1, f)), _resident((f, D_MODEL))]
    w_args = [gain.reshape(1, D_MODEL), wg, wu, cw, cb.reshape(1, f), wd]
    if sample:
        tm = n
        t = jnp.arange(n) % rows_per_seq
        h_rep = jnp.repeat(hist, rows_per_seq, axis=0)
        h1 = h_rep[:, 1]
        h2 = jnp.where((t == 0)[:, None], h_rep[:, 0], h_rep[:, 1])
        y, g = pl.pallas_call(
            functools.partial(_ffn_body, tm=tm, sample=True, tiles_per_seq=rows_per_seq),
            grid=(1,),
            in_specs=[pl.BlockSpec((tm, D_MODEL), row)] + w_specs
                     + [pl.BlockSpec((tm, f), row), pl.BlockSpec((tm, f), row)],
            out_specs=[pl.BlockSpec((tm, D_MODEL), row), pl.BlockSpec((tm, f), row)],
            out_shape=[jax.ShapeDtypeStruct((n, D_MODEL), F32), jax.ShapeDtypeStruct((n, f), F32)],
            scratch_shapes=[pltpu.VMEM((tm + SUBLANES, f), F32)],
            compiler_params=_params(("arbitrary",)),
            name="conv_ffn_sample",
        )(x, *w_args, h1, h2)
        return y, g.reshape(n_seq, rows_per_seq, f)[:, rows_per_seq - (CONV_WIDTH - 1):]
    tm = min(256, rows_per_seq)
    tiles_per_seq = rows_per_seq // tm
    seq_blk = lambda i: (i // tiles_per_seq, 0, 0)
    y, new_state = pl.pallas_call(
        functools.partial(_ffn_body, tm=tm, sample=False, tiles_per_seq=tiles_per_seq),
        grid=(n // tm,),
        in_specs=[pl.BlockSpec((tm, D_MODEL), row)] + w_specs
                 + [pl.BlockSpec((None, CONV_WIDTH - 1, f), seq_blk)],
        out_specs=[pl.BlockSpec((tm, D_MODEL), row), pl.BlockSpec((None, CONV_WIDTH - 1, f), seq_blk)],
        out_shape=[jax.ShapeDtypeStruct((n, D_MODEL), F32),
                   jax.ShapeDtypeStruct((n_seq, CONV_WIDTH - 1, f), F32)],
        scratch_shapes=[pltpu.VMEM((tm + SUBLANES, f), F32)],
        compiler_params=_params(("arbitrary",)),
        name="conv_ffn_prompt",
    )(x, *w_args, hist)
    return y, new_state


def _cumsum_body(lft_ref, ct_ref, carry, *, tm):
    @pl.when(pl.program_id(1) == 0)
    def _():
        carry[...] = jnp.zeros_like(carry)

    r = lax.broadcasted_iota(jnp.int32, (tm, tm), 0)
    c = lax.broadcasted_iota(jnp.int32, (tm, tm), 1)
    upto = jnp.where(r <= c, 1.0, 0.0).astype(BF16)
    out = _dot_split_lhs(lft_ref[...], upto, SPLIT_F32) + carry[...]
    ct_ref[...] = out
    carry[...] = out[:, tm - 1:tm]


def _seq_cumsum(lft):
    batch, _, seq = lft.shape
    tm = min(512, seq)
    spec = pl.BlockSpec((None, N_HEADS, tm), lambda b, i: (b, 0, i))
    return pl.pallas_call(
        functools.partial(_cumsum_body, tm=tm),
        grid=(batch, seq // tm),
        in_specs=[spec], out_specs=spec,
        out_shape=jax.ShapeDtypeStruct(lft.shape, F32),
        scratch_shapes=[pltpu.VMEM((N_HEADS, 1), F32)],
        compiler_params=_params(("parallel", "arbitrary")),
        name="logf_cumsum",
    )(lft)


def _top_k_rows(gate, k, n_valid):
    idx_of = lax.broadcasted_iota(jnp.int32, gate.shape, 0)
    sel = jnp.zeros(gate.shape, jnp.bool_)
    for r in range(k):
        mx = jnp.max(gate, axis=0, keepdims=True)
        idx = jnp.min(jnp.where(gate == mx, idx_of, gate.shape[0]), axis=0, keepdims=True)
        hit = idx_of == idx
        sel = jnp.logical_or(sel, jnp.logical_and(hit, r < n_valid))
        gate = jnp.where(hit, -jnp.inf, gate)
    return sel


def _stack_rows(blocks, single_rows, width):
    used = sum(p.shape[0] for p in blocks)
    out = jnp.concatenate(blocks + [jnp.zeros((LANES - used, width), F32)], axis=0)
    row = lax.broadcasted_iota(jnp.int32, (LANES, width), 0)
    for k, r in enumerate(single_rows):
        out = jnp.where(row == used + k, r, out)
    return out


_MOBA_FEATURE_ROWS = 16


def _prompt_attn_body(*refs, fox):
    qi_ref, kn_ref = refs[:2]
    if fox:
        q_ref, kt_ref, vt_ref, ctq_ref, ctk_ref, o_ref, qa, m_scr, l_scr, acc = refs[2:]
    else:
        q_ref, kt_ref, vt_ref, km_ref, o_ref, qa, m_scr, l_scr, acc = refs[2:]
    t = ATTN_TILE
    step = pl.program_id(1)
    i = qi_ref[step]
    n = kn_ref[step]
    scale = HEAD_DIM ** -0.5 * LOG2E
    lane = lax.broadcasted_iota(jnp.int32, (t, LANES), 1)
    lane_head = lane // HEAD_DIM
    one_row = jnp.ones((1, t), F32)

    @pl.when(n == 0)
    def _():
        m_scr[...] = jnp.full(m_scr.shape, NEG, F32)
        l_scr[...] = jnp.zeros(l_scr.shape, F32)
        acc[...] = jnp.zeros(acc.shape, F32)
        if fox:
            feat = _stack_rows(_split_f32(ctq_ref[...] * LOG2E, SPLIT_F32)
                               + [jnp.ones((SPLIT_F32 * N_HEADS, t), F32)], [], t).T
        else:
            t_in_tile = lax.broadcasted_iota(jnp.int32, (1, t), 1).astype(F32)
            blk_of_row = lax.broadcasted_iota(jnp.int32, (_MOBA_FEATURE_ROWS, t), 0)
        for h in range(N_HEADS):
            hp, e = divmod(h, HEADS_PER_TILE)
            q2 = q_ref[:, hp * LANES:(hp + 1) * LANES]
            qh = jnp.where(lane_head == e, q2, 0.0)
            if fox:
                mine = jnp.logical_and(lane % N_HEADS == h, lane < 2 * SPLIT_F32 * N_HEADS)
                feat_h = jnp.where(mine, feat, 0.0)
            else:
                km2 = km_ref[:, hp * LANES:(hp + 1) * LANES]
                q_hi, q_lo = _split(qh, 2)
                k_hi, k_lo = _split(km2, 2)
                gate = _dot_nt(k_hi, q_hi) + (_dot_nt(k_hi, q_lo) + _dot_nt(k_lo, q_hi))
                gate = jnp.where(blk_of_row < i, gate, -jnp.inf)
                sel = _top_k_rows(gate, MOBA_TOPK, i)
                sel_bias = jnp.where(jnp.logical_or(sel, blk_of_row >= i), 0.0, NEG)
                slope = _SLOPES[h] * LOG2E
                feat_h = _stack_rows(
                    [sel_bias],
                    _split_f32(-slope * t_in_tile, SPLIT_F32)
                    + _split_f32(slope * one_row, SPLIT_F32)
                    + _split_f32((-slope * t) * one_row, SPLIT_F32),
                    t).T
            qa[h] = jnp.concatenate([(qh * scale).astype(BF16), feat_h.astype(BF16)], axis=-1)

    if fox:
        neg_cs = [-p for p in _split_f32(ctk_ref[...] * LOG2E, SPLIT_F32)]
        k_feat = _stack_rows([jnp.ones((SPLIT_F32 * N_HEADS, t), F32)] + neg_cs, [], t).astype(BF16)
    else:
        key_in_tile = lax.broadcasted_iota(jnp.int32, (1, t), 1).astype(F32)
        onehot = jnp.where(lax.broadcasted_iota(jnp.int32, (_MOBA_FEATURE_ROWS, t), 0) == n, 1.0, 0.0)
        k_feat = _stack_rows([onehot], [one_row] * SPLIT_F32 + [key_in_tile] * SPLIT_F32
                             + [one_row * (i - n).astype(F32)] * SPLIT_F32, t).astype(BF16)

    def tile_pass(diagonal):
        if diagonal:
            row = lax.broadcasted_iota(jnp.int32, (Q_CHUNK, t), 0)
            col = lax.broadcasted_iota(jnp.int32, (Q_CHUNK, t), 1)
            allow = [row + c * Q_CHUNK >= col for c in range(t // Q_CHUNK)]
        first = _head_of_lane((Q_CHUNK, LANES), 1) == 0
        for hp in range(N_HEAD_PAIRS):
            sl = slice(hp * LANES, (hp + 1) * LANES)
            k2 = jnp.concatenate([kt_ref[sl, :], k_feat], axis=0)
            v2 = vt_ref[sl, :]
            alphas, pvs = [], []
            for e in range(HEADS_PER_TILE):
                h = hp * HEADS_PER_TILE + e
                alpha_c, pv_c = [], []
                for c in range(t // Q_CHUNK):
                    rows = slice(c * Q_CHUNK, (c + 1) * Q_CHUNK)
                    s = _dot(qa[h, rows, :], k2)
                    if diagonal:
                        s = jnp.where(allow[c], s, NEG)
                    m_prev = m_scr[h, rows, :]
                    m_new = jnp.maximum(m_prev, jnp.broadcast_to(jnp.max(s, axis=-1, keepdims=True), m_prev.shape))
                    alpha = jnp.exp2(m_prev - m_new)
                    p = jnp.concatenate([jnp.exp2(s[:, j * LANES:(j + 1) * LANES] - m_new)
                                         for j in range(t // LANES)], axis=-1)
                    l_scr[h, rows, :] = (alpha * l_scr[h, rows, :]
                                         + jnp.broadcast_to(jnp.sum(p, axis=-1, keepdims=True), m_prev.shape))
                    m_scr[h, rows, :] = m_new
                    alpha_c.append(alpha)
                    pv_c.append(_dot_nt(p.astype(BF16), v2))
                alphas.append(alpha_c)
                pvs.append(pv_c)
            for c in range(t // Q_CHUNK):
                rows = slice(c * Q_CHUNK, (c + 1) * Q_CHUNK)
                acc[rows, sl] = (acc[rows, sl] * jnp.where(first, alphas[0][c], alphas[1][c])
                                 + jnp.where(first, pvs[0][c], pvs[1][c]))

    @pl.when(n != i)
    def _():
        tile_pass(False)

    @pl.when(n == i)
    def _():
        tile_pass(True)
        for hp in range(N_HEAD_PAIRS):
            sl = slice(hp * LANES, (hp + 1) * LANES)
            l2 = jnp.where(lane_head == 0, l_scr[hp * HEADS_PER_TILE], l_scr[hp * HEADS_PER_TILE + 1])
            o_ref[:, sl] = acc[:, sl] / l2


def _prompt_attn(q, ktb, vtb, *, kmean=None, ct=None):
    batch, _, seq = ktb.shape
    n = q.shape[0]
    t = ATTN_TILE
    nq = seq // t
    fox = ct is not None
    assert fox or nq <= _MOBA_FEATURE_ROWS
    pairs = [(i, j) for i in range(nq) for j in range(i + 1)]
    qi = jnp.asarray([p[0] for p in pairs], jnp.int32)
    kn = jnp.asarray([p[1] for p in pairs], jnp.int32)
    q_map = lambda b, s, qi, kn: (b * nq + qi[s], 0)
    k_map = lambda b, s, qi, kn: (b, 0, kn[s])
    in_specs = [pl.BlockSpec((t, D_MODEL), q_map), pl.BlockSpec((None, D_MODEL, t), k_map),
                pl.BlockSpec((None, D_MODEL, t), k_map)]
    args = [q, ktb, vtb]
    if fox:
        in_specs += [pl.BlockSpec((None, N_HEADS, t), lambda b, s, qi, kn: (b, 0, qi[s])),
                     pl.BlockSpec((None, N_HEADS, t), k_map)]
        args += [ct, ct]
    else:
        in_specs += [pl.BlockSpec((None, _MOBA_FEATURE_ROWS, D_MODEL), lambda b, s, qi, kn: (b, 0, 0))]
        pad = _MOBA_FEATURE_ROWS - kmean.shape[1]
        args += [jnp.pad(kmean, ((0, 0), (0, pad), (0, 0))) if pad else kmean]
    return pl.pallas_call(
        functools.partial(_prompt_attn_body, fox=fox),
        grid_spec=pltpu.PrefetchScalarGridSpec(
            num_scalar_prefetch=2,
            grid=(batch, len(pairs)),
            in_specs=in_specs,
            out_specs=pl.BlockSpec((t, D_MODEL), q_map),
            scratch_shapes=[pltpu.VMEM((N_HEADS, t, 2 * LANES), BF16),
                            pltpu.VMEM((N_HEADS, t, LANES), F32),
                            pltpu.VMEM((N_HEADS, t, LANES), F32),
                            pltpu.VMEM((t, D_MODEL), F32)]),
        out_shape=jax.ShapeDtypeStruct((n, D_MODEL), F32),
        compiler_params=_params(("parallel", "arbitrary")),
        name="fox_prompt_attn" if fox else "moba_prompt_attn",
    )(qi, kn, *args)


def _sample_attn_body(*refs, fox, t_new, n_pages):
    g_pages = PAGES_PER_STEP
    ck_refs = refs[4:4 + g_pages]
    cv_refs = refs[4 + g_pages:4 + 2 * g_pages]
    q_ref, kn_ref, vn_ref = refs[1:4]
    rest = refs[4 + 2 * g_pages:]
    if fox:
        clf_refs = rest[:g_pages]
        lfn_ref, o_ref, qx, kpad, vpad, s_scr, acc, l_scr, carry = rest[g_pages:]
    else:
        o_ref, qx, kpad, vpad, s_scr, acc, l_scr, kmt = rest
    rows = t_new * N_HEADS
    past = n_pages * PAGE_SIZE
    n_groups = n_pages // g_pages
    phase = pl.program_id(1)
    p = pl.program_id(2)
    scale = HEAD_DIM ** -0.5
    row_col = lax.broadcasted_iota(jnp.int32, (rows, 1), 0)
    tok_col = row_col // N_HEADS
    head_col = row_col % N_HEADS

    @pl.when(jnp.logical_and(phase == 0, p == 0))
    def _():
        head_mask = _head_of_lane((N_HEADS, D_MODEL), 1) == lax.broadcasted_iota(jnp.int32, (N_HEADS, D_MODEL), 0)
        for t in range(t_new):
            qrow = jnp.broadcast_to(q_ref[t:t + 1, :], (N_HEADS, D_MODEL))
            qx[t * N_HEADS:(t + 1) * N_HEADS, :] = jnp.where(head_mask, qrow * scale, 0.0)
        kpad[...] = jnp.zeros(kpad.shape, BF16)
        vpad[...] = jnp.zeros(vpad.shape, BF16)
        kpad[0:t_new, :] = kn_ref[...].astype(BF16)
        vpad[0:t_new, :] = vn_ref[...].astype(BF16)
        s_scr[:, past:past + PAGE_SIZE] = _dot_nt(qx[...].astype(BF16), kpad[...])
        if fox:
            carry[...] = jnp.zeros(carry.shape, F32)
        else:
            kmt[...] = jnp.zeros(kmt.shape, F32)

    @pl.when(phase == 0)
    def _():
        group = n_groups - 1 - p
        qb = qx[...].astype(BF16)
        if fox:
            r0 = lax.broadcasted_iota(jnp.int32, (PAGE_SIZE, PAGE_SIZE), 0)
            r1 = lax.broadcasted_iota(jnp.int32, (PAGE_SIZE, PAGE_SIZE), 1)
            later = jnp.where(r0 > r1, 1.0, 0.0).astype(BF16)
        else:
            pages_per_block = MOBA_BLOCK // PAGE_SIZE
            lane_blk = lax.broadcasted_iota(jnp.int32, (D_MODEL, LANES), 1)
        for g in reversed(range(g_pages)):
            pk = group * g_pages + g
            kpage = ck_refs[g][...]
            s = _dot(qb, kpage.astype(BF16))
            if fox:
                lft = clf_refs[g][...]
                suffix = _dot_split_lhs(lft, later, SPLIT_F32) + carry[...]
                carry[...] = carry[...] + jnp.sum(lft, axis=-1, keepdims=True)
                s = s + jnp.concatenate([suffix] * t_new, axis=0)
            elif g % pages_per_block == 0:
                block_sum = kpage
                for g2 in range(g + 1, g + pages_per_block):
                    block_sum = block_sum + ck_refs[g2][...]
                mean = jnp.sum(block_sum, axis=-1, keepdims=True) * (1.0 / MOBA_BLOCK)
                kmt[...] = jnp.where(lane_blk == pk // pages_per_block, mean, kmt[...])
            s_scr[:, pl.ds(pl.multiple_of(pk * PAGE_SIZE, PAGE_SIZE), PAGE_SIZE)] = s

    @pl.when(jnp.logical_and(phase == 1, p == 0))
    def _():
        chunk = 2048 if past % 2048 == 0 else PAGE_SIZE
        n_chunks = past // chunk
        lane_c = lax.broadcasted_iota(jnp.int32, (rows, chunk), 1)
        lane_p = lax.broadcasted_iota(jnp.int32, (rows, PAGE_SIZE), 1)
        own = s_scr[:, past:past + PAGE_SIZE]
        own_ok = jnp.logical_and(lane_p <= tok_col, lane_p < t_new)
        if fox:
            lane_h = lax.broadcasted_iota(jnp.int32, (N_HEADS, N_HEADS), 1)
            sub_h = lax.broadcasted_iota(jnp.int32, (N_HEADS, N_HEADS), 0)
            cum = []
            for t in range(t_new):
                lrow = jnp.broadcast_to(lfn_ref[t:t + 1, :], (N_HEADS, N_HEADS))
                col = jnp.sum(jnp.where(lane_h == sub_h, lrow, 0.0), axis=-1, keepdims=True)
                cum.append(col if t == 0 else cum[-1] + col)
            c_new = jnp.concatenate(cum, axis=0)
            c_key = jnp.zeros((rows, PAGE_SIZE), F32)
            for t in range(t_new):
                c_key = jnp.where(lane_p == t, jnp.concatenate([cum[t]] * t_new, axis=0), c_key)
            own = jnp.where(own_ok, own + (c_new - c_key), NEG)
        else:
            slope = jnp.zeros((rows, 1), F32)
            for h in range(N_HEADS):
                slope = jnp.where(head_col == h, _SLOPES[h], slope)
            t_abs = (past + tok_col).astype(F32)
            n_full = past // MOBA_BLOCK
            q_hi, q_lo = _split(qx[...], 2)
            k_hi, k_lo = _split(kmt[...], 2)
            gate = _dot(q_hi, k_hi) + (_dot(q_hi, k_lo) + _dot(q_lo, k_hi))
            gate = jnp.where(lane_p < n_full, gate, -jnp.inf)
            sel = _top_k_lanes(gate, MOBA_TOPK, n_full)
            sel_bias = jnp.where(sel, 0.0, NEG).astype(BF16)
            own = jnp.where(own_ok, own - slope * (tok_col - lane_p).astype(F32), NEG)
        s_scr[:, past:past + PAGE_SIZE] = own
        m0 = jnp.max(own, axis=-1, keepdims=True)

        def bias_pass(ci, m):
            off = pl.multiple_of(ci * chunk, chunk)
            s = s_scr[:, pl.ds(off, chunk)]
            if fox:
                s = s + c_new
            else:
                blk = lax.broadcasted_iota(jnp.int32, (LANES, chunk), 0)
                key_blk = (lax.broadcasted_iota(jnp.int32, (LANES, chunk), 1) + ci * chunk) // MOBA_BLOCK
                expand = jnp.where(blk == key_blk, 1.0, 0.0).astype(BF16)
                s = s + _dot(sel_bias, expand)
                s = s - slope * (t_abs - (lane_c + ci * chunk).astype(F32))
            s_scr[:, pl.ds(off, chunk)] = s
            return jnp.maximum(m, jnp.max(s, axis=-1, keepdims=True))

        m = lax.fori_loop(0, n_chunks, bias_pass, m0)

        def exp_pass(ci, l):
            off = pl.multiple_of(ci * chunk, chunk)
            pexp = jnp.exp(s_scr[:, pl.ds(off, chunk)] - m)
            s_scr[:, pl.ds(off, chunk)] = pexp
            return l + jnp.sum(pexp, axis=-1, keepdims=True)

        p_own = jnp.exp(own - m)
        l = lax.fori_loop(0, n_chunks, exp_pass, jnp.sum(p_own, axis=-1, keepdims=True))
        l_scr[...] = l
        acc[...] = _dot(p_own.astype(BF16), vpad[...])

    @pl.when(phase == 1)
    def _():
        total = acc[...]
        for g in range(g_pages):
            off = pl.multiple_of((p * g_pages + g) * PAGE_SIZE, PAGE_SIZE)
            prob = s_scr[:, pl.ds(off, PAGE_SIZE)].astype(BF16)
            total = total + _dot_nt(prob, cv_refs[g][...].astype(BF16))
        acc[...] = total

    @pl.when(jnp.logical_and(phase == 1, p == n_groups - 1))
    def _():
        head_mask = _head_of_lane((rows, D_MODEL), 1) == lax.broadcasted_iota(jnp.int32, (rows, D_MODEL), 0) % N_HEADS
        out = jnp.where(head_mask, acc[...] / l_scr[...], 0.0)
        for t in range(t_new):
            o_ref[t:t + 1, :] = jnp.sum(out[t * N_HEADS:(t + 1) * N_HEADS, :], axis=0, keepdims=True)


def _top_k_lanes(gate, k, n_valid):
    lane = lax.broadcasted_iota(jnp.int32, gate.shape, 1)
    sel = jnp.zeros(gate.shape, jnp.bool_)
    for r in range(k):
        mx = jnp.max(gate, axis=-1, keepdims=True)
        idx = jnp.min(jnp.where(gate == mx, lane, LANES), axis=-1, keepdims=True)
        hit = lane == idx
        sel = jnp.logical_or(sel, jnp.logical_and(hit, r < n_valid))
        gate = jnp.where(hit, -jnp.inf, gate)
    return sel


def _sample_attn(q, k_new, v_new, cache_kt, cache_vt, layer, page_table, cache_lft=None, lf_new=None):
    n_seq, t_new, _ = q.shape
    n_pages = page_table.shape[1]
    g_pages = PAGES_PER_STEP
    assert n_pages % g_pages == 0 and (n_pages * PAGE_SIZE) % MOBA_BLOCK == 0
    n_groups = n_pages // g_pages
    fox = cache_lft is not None
    rows = t_new * N_HEADS
    past = n_pages * PAGE_SIZE
    seq_blk = lambda s, ph, p, pt: (s, 0, 0)

    def k_map(g):
        return lambda s, ph, p, pt: (
            layer, pt[s * n_pages + jnp.where(ph == 0, n_groups - 1 - p, 0) * g_pages + g], 0, 0)

    def v_map(g):
        return lambda s, ph, p, pt: (layer, pt[s * n_pages + jnp.where(ph == 0, 0, p) * g_pages + g], 0, 0)

    in_specs = ([pl.BlockSpec((None, t_new, D_MODEL), seq_blk)] * 3
                + [pl.BlockSpec((None, None, D_MODEL, PAGE_SIZE), k_map(g)) for g in range(g_pages)]
                + [pl.BlockSpec((None, None, D_MODEL, PAGE_SIZE), v_map(g)) for g in range(g_pages)])
    args = [q, k_new, v_new] + [cache_kt] * g_pages + [cache_vt] * g_pages
    scratch = [pltpu.VMEM((rows, D_MODEL), F32),
               pltpu.VMEM((PAGE_SIZE, D_MODEL), BF16),
               pltpu.VMEM((PAGE_SIZE, D_MODEL), BF16),
               pltpu.VMEM((rows, past + PAGE_SIZE), F32),
               pltpu.VMEM((rows, D_MODEL), F32),
               pltpu.VMEM((rows, 1), F32)]
    if fox:
        in_specs += [pl.BlockSpec((None, None, N_HEADS, PAGE_SIZE), k_map(g)) for g in range(g_pages)]
        in_specs += [pl.BlockSpec((None, t_new, N_HEADS), seq_blk)]
        args += [cache_lft] * g_pages + [lf_new]
        scratch += [pltpu.VMEM((N_HEADS, 1), F32)]
    else:
        scratch += [pltpu.VMEM((D_MODEL, LANES), F32)]
    return pl.pallas_call(
        functools.partial(_sample_attn_body, fox=fox, t_new=t_new, n_pages=n_pages),
        grid_spec=pltpu.PrefetchScalarGridSpec(
            num_scalar_prefetch=1,
            grid=(n_seq, 2, n_groups),
            in_specs=in_specs,
            out_specs=pl.BlockSpec((None, t_new, D_MODEL), seq_blk),
            scratch_shapes=scratch),
        out_shape=jax.ShapeDtypeStruct((n_seq, t_new, D_MODEL), F32),
        compiler_params=_params(("parallel", "arbitrary", "arbitrary")),
        name="fox_sample_attn" if fox else "moba_sample_attn",
    )(page_table.reshape(-1), *args)


def kernel(x_prompt, x_sample, cache_k_moba, cache_v_moba, cache_k_fox, cache_v_fox, cache_logf_fox, state_conv, page_table, attn_norm, ffn_norm, moba_w_qkv, moba_q_gain, moba_k_gain, moba_w_o, fox_w_qkvf, fox_b_f, fox_q_gain, fox_k_gain, fox_w_o, ffn_w_gate, ffn_w_up, ffn_conv_w, ffn_conv_b, ffn_w_down):
    b, n, d = x_prompt.shape
    db, t_new, _ = x_sample.shape
    depth = attn_norm.shape[0]
    f = ffn_w_gate.shape[-1]
    page_t = lambda a: a.transpose(0, 1, 3, 4, 2).reshape(a.shape[0], a.shape[1], D_MODEL, PAGE_SIZE)
    cache_k_moba, cache_v_moba = page_t(cache_k_moba), page_t(cache_v_moba)
    cache_k_fox, cache_v_fox = page_t(cache_k_fox), page_t(cache_v_fox)
    cache_lft = cache_logf_fox.transpose(0, 1, 3, 2)
    heads = lambda a, lead: a.reshape(lead + (N_HEADS, HEAD_DIM))
    heads_t = lambda a: a.reshape(b, N_HEADS, HEAD_DIM, n).transpose(0, 3, 1, 2)
    seq3 = lambda a: a.reshape(db, t_new, a.shape[-1])

    yp = x_prompt.reshape(b * n, d)
    ys = x_sample.reshape(db * t_new, d)
    moba_out = [[] for _ in range(4)]
    fox_out = [[] for _ in range(6)]
    conv_p, conv_s = [], []
    zero_hist = jnp.zeros((b, CONV_WIDTH - 1, f), F32)
    for layer in range(depth):
        j = layer // 2
        if layer % 2 == 0:
            w = moba_w_qkv[j].astype(BF16)
            qp, kt, ktb, vt, vtb, kmean = _qkv_prompt(yp, attn_norm[layer], w, moba_q_gain[j], moba_k_gain[j], batch=b)
            qs, ks, vs = _qkv_rows(ys, attn_norm[layer], w, moba_q_gain[j], moba_k_gain[j])
            mix_p = _prompt_attn(qp, ktb, vtb, kmean=kmean.reshape(b, n // MOBA_BLOCK, d))
            mix_s = _sample_attn(seq3(qs), seq3(ks), seq3(vs), cache_k_moba, cache_v_moba, j, page_table)
            w_o = moba_w_o[j]
            for lst, a in zip(moba_out, (heads_t(kt), heads_t(vt), heads(ks, (db, t_new)), heads(vs, (db, t_new)))):
                lst.append(a)
        else:
            w = fox_w_qkvf[j, :, :3 * d].astype(BF16)
            wf = jnp.pad(fox_w_qkvf[j, :, 3 * d:], ((0, 0), (0, LANES - N_HEADS))).astype(BF16)
            bf = jnp.pad(fox_b_f[j], (0, LANES - N_HEADS)).reshape(1, LANES)
            qp, kt, ktb, vt, vtb, lft = _qkv_prompt(yp, attn_norm[layer], w, fox_q_gain[j], fox_k_gain[j],
                                                    batch=b, wf=wf, bf=bf)
            qs, ks, vs, ls = _qkv_rows(ys, attn_norm[layer], w, fox_q_gain[j], fox_k_gain[j], wf, bf)
            mix_p = _prompt_attn(qp, ktb, vtb, ct=_seq_cumsum(lft))
            mix_s = _sample_attn(seq3(qs), seq3(ks), seq3(vs), cache_k_fox, cache_v_fox, j, page_table,
                                 cache_lft=cache_lft, lf_new=seq3(ls))
            w_o = fox_w_o[j]
            for lst, a in zip(fox_out, (heads_t(kt), heads_t(vt), lft.transpose(0, 2, 1),
                                        heads(ks, (db, t_new)), heads(vs, (db, t_new)), seq3(ls))):
                lst.append(a)
        w_o = w_o.astype(BF16)
        yp = _wo_proj(yp, mix_p, w_o)
        ys = _wo_proj(ys, mix_s.reshape(db * t_new, d), w_o)
        ffn_w = (ffn_norm[layer], ffn_w_gate[layer].astype(BF16), ffn_w_up[layer].astype(BF16),
                 ffn_conv_w[layer], ffn_conv_b[layer], ffn_w_down[layer].astype(BF16))
        yp, cp = _conv_ffn(yp, zero_hist, *ffn_w, rows_per_seq=n)
        ys, cs = _conv_ffn(ys, state_conv[layer], *ffn_w, rows_per_seq=t_new)
        conv_p.append(cp)
        conv_s.append(cs)
    return (yp.reshape(b, n, d), ys.reshape(db, t_new, d),
            *[jnp.stack(a) for a in moba_out],
            *[jnp.stack(a) for a in fox_out],
            jnp.stack(conv_p), jnp.stack(conv_s))
```

```python
import functools

import numpy as np
import jax
import jax.numpy as jnp
from jax import lax
from jax.experimental import pallas as pl
from jax.experimental.pallas import tpu as pltpu

N_HEADS = 16
HEAD_DIM = 64
D_MODEL = N_HEADS * HEAD_DIM
MOBA_BLOCK = 256
MOBA_TOPK = 3
PAGE_SIZE = 128
CONV_WIDTH = 3
RMS_EPS = 1e-6
LANES = 128
SUBLANES = 8
HEADS_PER_TILE = LANES // HEAD_DIM
N_HEAD_PAIRS = N_HEADS // HEADS_PER_TILE
VMEM_LIMIT = 56 * 1024 * 1024
NEG = -1e30
ATTN_TILE = MOBA_BLOCK
Q_CHUNK = 128
RING_PAGES = 16
ITER_PAGES = RING_PAGES // 2
SPLIT_F32 = 3
LOG2E = 1.4426950408889634

F32 = jnp.float32
BF16 = jnp.bfloat16

_SLOPES = [float(s) for s in np.exp2(np.float32(-8.0) * np.arange(1, N_HEADS + 1, dtype=np.float32)
                                     / np.float32(N_HEADS)).astype(np.float32)]


def _dot(a, b):
    return jnp.dot(a, b, preferred_element_type=F32)


def _dot_nt(a, b):
    return lax.dot_general(a, b, (((1,), (1,)), ((), ())), preferred_element_type=F32)


def _split(a, terms):
    parts = []
    r = a
    for _ in range(terms):
        p = r.astype(BF16)
        parts.append(p)
        r = r - p.astype(F32)
    return parts


def _split_f32(a, terms):
    return [p.astype(F32) for p in _split(a, terms)]


def _dot_split_lhs(a, b_bf16, terms):
    out = None
    for p in _split(a, terms):
        d = _dot(p, b_bf16)
        out = d if out is None else out + d
    return out


def _dot_split_rhs(a_bf16, b, terms):
    out = None
    for p in _split(b, terms):
        d = _dot(a_bf16, p)
        out = d if out is None else out + d
    return out


def _params(sem):
    return pltpu.CompilerParams(dimension_semantics=sem, vmem_limit_bytes=VMEM_LIMIT)


def _resident(shape):
    return pl.BlockSpec(shape, lambda *_: (0,) * len(shape), pipeline_mode=pl.Buffered(1))


def _head_of_lane(shape, dim):
    return lax.broadcasted_iota(jnp.int32, shape, dim) // HEAD_DIM


def _log_sigmoid(f):
    return -(jnp.maximum(-f, 0.0) + jnp.log1p(jnp.exp(-jnp.abs(f))))


def _seg_mats():
    lane_head = np.arange(D_MODEL) // HEAD_DIM
    seg = (lane_head[:, None] == np.arange(LANES)[None, :]).astype(np.float32)
    return jnp.asarray(seg, BF16), jnp.asarray(seg.T, BF16)


def _rms_rows(x, gain):
    ms = jnp.mean(x * x, axis=-1, keepdims=True)
    return ((x * lax.rsqrt(ms + RMS_EPS)) * gain).astype(BF16)


def _head_norm_rows(a, gain, seg, segt):
    ss = _dot_split_lhs(a * a, seg, 2)
    rs = lax.rsqrt(ss * (1.0 / HEAD_DIM) + RMS_EPS)
    return (a * _dot_split_lhs(rs, segt, 2)) * gain


def _qkv_rows_body(*refs, fox):
    x_ref, g_ref, w_ref, qg_ref, kg_ref, seg_ref, segt_ref = refs[:7]
    if fox:
        wf_ref, bf_ref, q_ref, k_ref, v_ref, lf_ref = refs[7:]
    else:
        q_ref, k_ref, v_ref = refs[7:]
    h = _rms_rows(x_ref[...], g_ref[...])
    qkv = _dot(h, w_ref[...])
    seg = seg_ref[...]
    segt = segt_ref[...]
    q_ref[...] = _head_norm_rows(qkv[:, :D_MODEL], qg_ref[...], seg, segt)
    k_ref[...] = _head_norm_rows(qkv[:, D_MODEL:2 * D_MODEL], kg_ref[...], seg, segt)
    v_ref[...] = qkv[:, 2 * D_MODEL:]
    if fox:
        lf_ref[...] = _log_sigmoid(_dot(h, wf_ref[...]) + bf_ref[...])[:, :N_HEADS]


def _qkv_rows(x, gain, w, q_gain, k_gain, wf=None, bf=None):
    n = x.shape[0]
    tm = min(MOBA_BLOCK, n)
    fox = wf is not None
    seg, segt = _seg_mats()
    row = lambda i: (i, 0)
    in_specs = [pl.BlockSpec((tm, D_MODEL), row), _resident((1, D_MODEL)), _resident((D_MODEL, 3 * D_MODEL)),
                _resident((1, D_MODEL)), _resident((1, D_MODEL)),
                _resident((D_MODEL, LANES)), _resident((LANES, D_MODEL))]
    args = [x, gain.reshape(1, D_MODEL), w,
            jnp.tile(q_gain, N_HEADS).reshape(1, D_MODEL), jnp.tile(k_gain, N_HEADS).reshape(1, D_MODEL),
            seg, segt]
    out_shape = [jax.ShapeDtypeStruct((n, D_MODEL), F32)] * 3
    out_specs = [pl.BlockSpec((tm, D_MODEL), row)] * 3
    if fox:
        in_specs += [_resident((D_MODEL, LANES)), _resident((1, LANES))]
        args += [wf, bf]
        out_shape.append(jax.ShapeDtypeStruct((n, N_HEADS), F32))
        out_specs.append(pl.BlockSpec((tm, N_HEADS), row))
    return pl.pallas_call(
        functools.partial(_qkv_rows_body, fox=fox),
        grid=(n // tm,),
        in_specs=in_specs, out_specs=out_specs, out_shape=out_shape,
        compiler_params=_params(("parallel",)),
        name="qkv_rows",
    )(*args)


def _head_norm_cols(at, gain_col):
    a3 = at.reshape(N_HEADS, HEAD_DIM, at.shape[1])
    ss = jnp.sum(a3 * a3, axis=1, keepdims=True)
    a3 = a3 * lax.rsqrt(ss * (1.0 / HEAD_DIM) + RMS_EPS)
    return a3.reshape(at.shape) * gain_col


def _qkv_prompt_body(*refs, fox):
    x_ref, g_ref, wqt_ref, wkt_ref, wvt_ref, qg_ref, kg_ref = refs[:7]
    if fox:
        wft_ref, bf_ref, q_ref, kt_ref, ktb_ref, vt_ref, vtb_ref, lft_ref = refs[7:]
    else:
        q_ref, kt_ref, ktb_ref, vt_ref, vtb_ref, km_ref = refs[7:]
    h = _rms_rows(x_ref[...], g_ref[...])
    q_ref[...] = _head_norm_cols(_dot_nt(wqt_ref[...], h), qg_ref[...]).T

    kt = _head_norm_cols(_dot_nt(wkt_ref[...], h), kg_ref[...])
    kt_ref[...] = kt
    ktb_ref[...] = kt.astype(BF16)
    vt = _dot_nt(wvt_ref[...], h)
    vt_ref[...] = vt
    vtb_ref[...] = vt.astype(BF16)
    if fox:
        lft_ref[...] = _log_sigmoid(_dot_nt(wft_ref[...], h) + bf_ref[...])[:N_HEADS, :]
    else:
        ones = jnp.ones((SUBLANES, kt.shape[1]), BF16)
        total = None
        for part in _split(kt, SPLIT_F32):
            d = _dot_nt(ones, part)
            total = d if total is None else total + d
        km_ref[...] = total[0:1, :] * (1.0 / MOBA_BLOCK)


def _qkv_prompt(x, gain, w, q_gain, k_gain, *, batch, wf=None, bf=None):
    n = x.shape[0]
    seq = n // batch
    tm = MOBA_BLOCK
    ns = seq // tm
    fox = wf is not None
    row = lambda b, i: (b * ns + i, 0)
    col = lambda b, i: (b, 0, i)
    wqt = w[:, :D_MODEL].T
    wkt = w[:, D_MODEL:2 * D_MODEL].T
    wvt = w[:, 2 * D_MODEL:].T
    in_specs = [pl.BlockSpec((tm, D_MODEL), row), _resident((1, D_MODEL)),
                _resident((D_MODEL, D_MODEL)), _resident((D_MODEL, D_MODEL)), _resident((D_MODEL, D_MODEL)),
                _resident((D_MODEL, 1)), _resident((D_MODEL, 1))]
    args = [x, gain.reshape(1, D_MODEL), wqt, wkt, wvt,
            jnp.tile(q_gain, N_HEADS).reshape(D_MODEL, 1), jnp.tile(k_gain, N_HEADS).reshape(D_MODEL, 1)]
    t_spec = pl.BlockSpec((None, D_MODEL, tm), col)
    t_f32 = jax.ShapeDtypeStruct((batch, D_MODEL, seq), F32)
    t_bf16 = jax.ShapeDtypeStruct((batch, D_MODEL, seq), BF16)
    out_shape = [jax.ShapeDtypeStruct((n, D_MODEL), F32), t_f32, t_bf16, t_f32, t_bf16]
    out_specs = [pl.BlockSpec((tm, D_MODEL), row), t_spec, t_spec, t_spec, t_spec]
    if fox:
        in_specs += [_resident((LANES, D_MODEL)), _resident((LANES, 1))]
        args += [wf.T, bf.reshape(LANES, 1)]
        out_shape.append(jax.ShapeDtypeStruct((batch, N_HEADS, seq), F32))
        out_specs.append(pl.BlockSpec((None, N_HEADS, tm), col))
    else:
        out_shape.append(jax.ShapeDtypeStruct((batch * ns, 1, D_MODEL), F32))
        out_specs.append(pl.BlockSpec((None, 1, D_MODEL), lambda b, i: (b * ns + i, 0, 0)))
    return pl.pallas_call(
        functools.partial(_qkv_prompt_body, fox=fox),
        grid=(batch, ns),
        in_specs=in_specs, out_specs=out_specs, out_shape=out_shape,
        compiler_params=_params(("parallel", "parallel")),
        name="qkv_prompt",
    )(*args)


def _wo_body(x_ref, a_ref, w_ref, o_ref):
    o_ref[...] = x_ref[...] + _dot(a_ref[...].astype(BF16), w_ref[...])


def _wo_proj(x, a, w):
    n = x.shape[0]
    tm = min(512, n)
    row = lambda i: (i, 0)
    return pl.pallas_call(
        _wo_body,
        grid=(n // tm,),
        in_specs=[pl.BlockSpec((tm, D_MODEL), row), pl.BlockSpec((tm, D_MODEL), row),
                  _resident((D_MODEL, D_MODEL))],
        out_specs=pl.BlockSpec((tm, D_MODEL), row),
        out_shape=jax.ShapeDtypeStruct((n, D_MODEL), F32),
        compiler_params=_params(("parallel",)),
        name="wo_proj",
    )(x, a, w)


def _ffn_body(*refs, tm, sample, tiles_per_seq):
    if sample:
        (x_ref, g_ref, wg_ref, wu_ref, cw_ref, cb_ref, wd_ref, h1_ref, h2_ref,
         y_ref, gout_ref, gbuf) = refs
    else:
        (x_ref, g_ref, wg_ref, wu_ref, cw_ref, cb_ref, wd_ref, hist_ref,
         y_ref, gout_ref, gbuf) = refs
    halo = SUBLANES
    x = x_ref[...]
    xn = _rms_rows(x, g_ref[...])
    g = _dot(xn, wg_ref[...])
    u = _dot(xn, wu_ref[...])
    if sample:
        gbuf[0:halo, :] = jnp.zeros((halo, g.shape[1]), F32)
    else:
        i = pl.program_id(0)

        @pl.when(i % tiles_per_seq != 0)
        def _():
            gbuf[0:halo, :] = gbuf[tm:tm + halo, :]

        @pl.when(i % tiles_per_seq == 0)
        def _():
            gbuf[halo - (CONV_WIDTH - 1):halo, :] = hist_ref[...]
    gbuf[halo:halo + tm, :] = g
    prev1 = gbuf[halo - 1:halo - 1 + tm, :]
    prev2 = gbuf[halo - 2:halo - 2 + tm, :]
    if sample:
        t = lax.broadcasted_iota(jnp.int32, (tm, 1), 0) % tiles_per_seq
        prev1 = jnp.where(t == 0, h1_ref[...], prev1)
        prev2 = jnp.where(t < 2, h2_ref[...], prev2)
        gout_ref[...] = g
    else:
        gout_ref[...] = g[tm - (CONV_WIDTH - 1):, :]
    cw = cw_ref[...]
    g_conv = cb_ref[...] + cw[0:1, :] * prev2
    g_conv = g_conv + cw[1:2, :] * prev1
    g_conv = g_conv + cw[2:3, :] * g
    hidden = (g_conv * jax.nn.sigmoid(g_conv)) * u
    y_ref[...] = x + _dot(hidden.astype(BF16), wd_ref[...])


def _conv_ffn(x, hist, gain, wg, wu, cw, cb, wd, *, rows_per_seq):
    n = x.shape[0]
    f = wg.shape[1]
    n_seq = n // rows_per_seq
    sample = rows_per_seq < SUBLANES
    row = lambda i: (i, 0)
    w_specs = [_resident((1, D_MODEL)), _resident((D_MODEL, f)), _resident((D_MODEL, f)),
               _resident((CONV_WIDTH, f)), _resident((1, f)), _resident((f, D_MODEL))]
    w_args = [gain.reshape(1, D_MODEL), wg, wu, cw, cb.reshape(1, f), wd]
    if sample:
        tm = n
        t = jnp.arange(n) % rows_per_seq
        h_rep = jnp.repeat(hist, rows_per_seq, axis=0)
        h1 = h_rep[:, 1]
        h2 = jnp.where((t == 0)[:, None], h_rep[:, 0], h_rep[:, 1])
        y, g = pl.pallas_call(
            functools.partial(_ffn_body, tm=tm, sample=True, tiles_per_seq=rows_per_seq),
            grid=(1,),
            in_specs=[pl.BlockSpec((tm, D_MODEL), row)] + w_specs
                     + [pl.BlockSpec((tm, f), row), pl.BlockSpec((tm, f), row)],
            out_specs=[pl.BlockSpec((tm, D_MODEL), row), pl.BlockSpec((tm, f), row)],
            out_shape=[jax.ShapeDtypeStruct((n, D_MODEL), F32), jax.ShapeDtypeStruct((n, f), F32)],
            scratch_shapes=[pltpu.VMEM((tm + SUBLANES, f), F32)],
            compiler_params=_params(("arbitrary",)),
            name="conv_ffn_sample",
        )(x, *w_args, h1, h2)
        return y, g.reshape(n_seq, rows_per_seq, f)[:, rows_per_seq - (CONV_WIDTH - 1):]
    tm = min(256, rows_per_seq)
    tiles_per_seq = rows_per_seq // tm
    seq_blk = lambda i: (i // tiles_per_seq, 0, 0)
    y, new_state = pl.pallas_call(
        functools.partial(_ffn_body, tm=tm, sample=False, tiles_per_seq=tiles_per_seq),
        grid=(n // tm,),
        in_specs=[pl.BlockSpec((tm, D_MODEL), row)] + w_specs
                 + [pl.BlockSpec((None, CONV_WIDTH - 1, f), seq_blk)],
        out_specs=[pl.BlockSpec((tm, D_MODEL), row), pl.BlockSpec((None, CONV_WIDTH - 1, f), seq_blk)],
        out_shape=[jax.ShapeDtypeStruct((n, D_MODEL), F32),
                   jax.ShapeDtypeStruct((n_seq, CONV_WIDTH - 1, f), F32)],
        scratch_shapes=[pltpu.VMEM((tm + SUBLANES, f), F32)],
        compiler_params=_params(("arbitrary",)),
        name="conv_ffn_prompt",
    )(x, *w_args, hist)
    return y, new_state


def _cumsum_body(lft_ref, ct_ref, carry, *, tm):
    @pl.when(pl.program_id(1) == 0)
    def _():
        carry[...] = jnp.zeros_like(carry)

    r = lax.broadcasted_iota(jnp.int32, (tm, tm), 0)
    c = lax.broadcasted_iota(jnp.int32, (tm, tm), 1)
    upto = jnp.where(r <= c, 1.0, 0.0).astype(BF16)
    out = _dot_split_lhs(lft_ref[...], upto, SPLIT_F32) + carry[...]
    ct_ref[...] = out
    carry[...] = out[:, tm - 1:tm]


def _seq_cumsum(lft):
    batch, _, seq = lft.shape
    tm = min(512, seq)
    spec = pl.BlockSpec((None, N_HEADS, tm), lambda b, i: (b, 0, i))
    return pl.pallas_call(
        functools.partial(_cumsum_body, tm=tm),
        grid=(batch, seq // tm),
        in_specs=[spec], out_specs=spec,
        out_shape=jax.ShapeDtypeStruct(lft.shape, F32),
        scratch_shapes=[pltpu.VMEM((N_HEADS, 1), F32)],
        compiler_params=_params(("parallel", "arbitrary")),
        name="logf_cumsum",
    )(lft)


def _top_k_rows(gate, k, n_valid):
    idx_of = lax.broadcasted_iota(jnp.int32, gate.shape, 0)
    sel = jnp.zeros(gate.shape, jnp.bool_)
    for r in range(k):
        mx = jnp.max(gate, axis=0, keepdims=True)
        idx = jnp.min(jnp.where(gate == mx, idx_of, gate.shape[0]), axis=0, keepdims=True)
        hit = idx_of == idx
        sel = jnp.logical_or(sel, jnp.logical_and(hit, r < n_valid))
        gate = jnp.where(hit, -jnp.inf, gate)
    return sel


def _stack_rows(blocks, single_rows, width):
    used = sum(p.shape[0] for p in blocks)
    out = jnp.concatenate(blocks + [jnp.zeros((LANES - used, width), F32)], axis=0)
    row = lax.broadcasted_iota(jnp.int32, (LANES, width), 0)
    for k, r in enumerate(single_rows):
        out = jnp.where(row == used + k, r, out)
    return out


_MOBA_FEATURE_ROWS = 16


def _prompt_attn_body(*refs, fox):
    qi_ref, kn_ref = refs[:2]
    if fox:
        q_ref, kt_ref, vt_ref, ctq_ref, ctk_ref, o_ref, qa, m_scr, l_scr, acc = refs[2:]
    else:
        q_ref, kt_ref, vt_ref, km_ref, o_ref, qa, m_scr, l_scr, acc = refs[2:]
    t = ATTN_TILE
    step = pl.program_id(1)
    i = qi_ref[step]
    n = kn_ref[step]
    scale = HEAD_DIM ** -0.5 * LOG2E
    lane = lax.broadcasted_iota(jnp.int32, (t, LANES), 1)
    lane_head = lane // HEAD_DIM
    one_row = jnp.ones((1, t), F32)

    @pl.when(n == 0)
    def _():
        m_scr[...] = jnp.full(m_scr.shape, NEG, F32)
        l_scr[...] = jnp.zeros(l_scr.shape, F32)
        acc[...] = jnp.zeros(acc.shape, F32)
        if fox:
            feat = _stack_rows(_split_f32(ctq_ref[...] * LOG2E, SPLIT_F32)
                               + [jnp.ones((SPLIT_F32 * N_HEADS, t), F32)], [], t).T
        else:
            t_in_tile = lax.broadcasted_iota(jnp.int32, (1, t), 1).astype(F32)
            blk_of_row = lax.broadcasted_iota(jnp.int32, (_MOBA_FEATURE_ROWS, t), 0)
        for h in range(N_HEADS):
            hp, e = divmod(h, HEADS_PER_TILE)
            q2 = q_ref[:, hp * LANES:(hp + 1) * LANES]
            qh = jnp.where(lane_head == e, q2, 0.0)
            if fox:
                mine = jnp.logical_and(lane % N_HEADS == h, lane < 2 * SPLIT_F32 * N_HEADS)
                feat_h = jnp.where(mine, feat, 0.0)
            else:
                km2 = km_ref[:, hp * LANES:(hp + 1) * LANES]
                q_hi, q_lo = _split(qh, 2)
                k_hi, k_lo = _split(km2, 2)
                gate = _dot_nt(k_hi, q_hi) + (_dot_nt(k_hi, q_lo) + _dot_nt(k_lo, q_hi))
                gate = jnp.where(blk_of_row < i, gate, -jnp.inf)
                sel = _top_k_rows(gate, MOBA_TOPK, i)
                sel_bias = jnp.where(jnp.logical_or(sel, blk_of_row >= i), 0.0, NEG)
                slope = _SLOPES[h] * LOG2E
                feat_h = _stack_rows(
                    [sel_bias],
                    _split_f32(-slope * t_in_tile, SPLIT_F32)
                    + _split_f32(slope * one_row, SPLIT_F32)
                    + _split_f32((-slope * t) * one_row, SPLIT_F32),
                    t).T
            qa[h] = jnp.concatenate([(qh * scale).astype(BF16), feat_h.astype(BF16)], axis=-1)

    if fox:
        neg_cs = [-p for p in _split_f32(ctk_ref[...] * LOG2E, SPLIT_F32)]
        k_feat = _stack_rows([jnp.ones((SPLIT_F32 * N_HEADS, t), F32)] + neg_cs, [], t).astype(BF16)
    else:
        key_in_tile = lax.broadcasted_iota(jnp.int32, (1, t), 1).astype(F32)
        onehot = jnp.where(lax.broadcasted_iota(jnp.int32, (_MOBA_FEATURE_ROWS, t), 0) == n, 1.0, 0.0)
        k_feat = _stack_rows([onehot], [one_row] * SPLIT_F32 + [key_in_tile] * SPLIT_F32
                             + [one_row * (i - n).astype(F32)] * SPLIT_F32, t).astype(BF16)

    def tile_pass(diagonal):
        if diagonal:
            row = lax.broadcasted_iota(jnp.int32, (Q_CHUNK, t), 0)
            col = lax.broadcasted_iota(jnp.int32, (Q_CHUNK, t), 1)
            allow = [row + c * Q_CHUNK >= col for c in range(t // Q_CHUNK)]
        first = _head_of_lane((Q_CHUNK, LANES), 1) == 0
        for hp in range(N_HEAD_PAIRS):
            sl = slice(hp * LANES, (hp + 1) * LANES)
            k2 = jnp.concatenate([kt_ref[sl, :], k_feat], axis=0)
            v2 = vt_ref[sl, :]
            alphas, pvs = [], []
            for e in range(HEADS_PER_TILE):
                h = hp * HEADS_PER_TILE + e
                alpha_c, pv_c = [], []
                for c in range(t // Q_CHUNK):
                    rows = slice(c * Q_CHUNK, (c + 1) * Q_CHUNK)
                    s = _dot(qa[h, rows, :], k2)
                    if diagonal:
                        s = jnp.where(allow[c], s, NEG)
                    m_prev = m_scr[h, rows, :]
                    m_new = jnp.maximum(m_prev, jnp.broadcast_to(jnp.max(s, axis=-1, keepdims=True), m_prev.shape))
                    alpha = jnp.exp2(m_prev - m_new)
                    p = jnp.concatenate([jnp.exp2(s[:, j * LANES:(j + 1) * LANES] - m_new)
                                         for j in range(t // LANES)], axis=-1)
                    l_scr[h, rows, :] = (alpha * l_scr[h, rows, :]
                                         + jnp.broadcast_to(jnp.sum(p, axis=-1, keepdims=True), m_prev.shape))
                    m_scr[h, rows, :] = m_new
                    alpha_c.append(alpha)
                    pv_c.append(_dot_nt(p.astype(BF16), v2))
                alphas.append(alpha_c)
                pvs.append(pv_c)
            for c in range(t // Q_CHUNK):
                rows = slice(c * Q_CHUNK, (c + 1) * Q_CHUNK)
                acc[rows, sl] = (acc[rows, sl] * jnp.where(first, alphas[0][c], alphas[1][c])
                                 + jnp.where(first, pvs[0][c], pvs[1][c]))

    @pl.when(n != i)
    def _():
        tile_pass(False)

    @pl.when(n == i)
    def _():
        tile_pass(True)
        for hp in range(N_HEAD_PAIRS):
            sl = slice(hp * LANES, (hp + 1) * LANES)
            l2 = jnp.where(lane_head == 0, l_scr[hp * HEADS_PER_TILE], l_scr[hp * HEADS_PER_TILE + 1])
            o_ref[:, sl] = acc[:, sl] / l2


def _prompt_attn(q, ktb, vtb, *, kmean=None, ct=None):
    batch, _, seq = ktb.shape
    n = q.shape[0]
    t = ATTN_TILE
    nq = seq // t
    fox = ct is not None
    assert fox or nq <= _MOBA_FEATURE_ROWS
    pairs = [(i, j) for i in range(nq) for j in range(i + 1)]
    qi = jnp.asarray([p[0] for p in pairs], jnp.int32)
    kn = jnp.asarray([p[1] for p in pairs], jnp.int32)
    q_map = lambda b, s, qi, kn: (b * nq + qi[s], 0)
    k_map = lambda b, s, qi, kn: (b, 0, kn[s])
    in_specs = [pl.BlockSpec((t, D_MODEL), q_map), pl.BlockSpec((None, D_MODEL, t), k_map),
                pl.BlockSpec((None, D_MODEL, t), k_map)]
    args = [q, ktb, vtb]
    if fox:
        in_specs += [pl.BlockSpec((None, N_HEADS, t), lambda b, s, qi, kn: (b, 0, qi[s])),
                     pl.BlockSpec((None, N_HEADS, t), k_map)]
        args += [ct, ct]
    else:
        in_specs += [pl.BlockSpec((None, _MOBA_FEATURE_ROWS, D_MODEL), lambda b, s, qi, kn: (b, 0, 0))]
        pad = _MOBA_FEATURE_ROWS - kmean.shape[1]
        args += [jnp.pad(kmean, ((0, 0), (0, pad), (0, 0))) if pad else kmean]
    return pl.pallas_call(
        functools.partial(_prompt_attn_body, fox=fox),
        grid_spec=pltpu.PrefetchScalarGridSpec(
            num_scalar_prefetch=2,
            grid=(batch, len(pairs)),
            in_specs=in_specs,
            out_specs=pl.BlockSpec((t, D_MODEL), q_map),
            scratch_shapes=[pltpu.VMEM((N_HEADS, t, 2 * LANES), BF16),
                            pltpu.VMEM((N_HEADS, t, LANES), F32),
                            pltpu.VMEM((N_HEADS, t, LANES), F32),
                            pltpu.VMEM((t, D_MODEL), F32)]),
        out_shape=jax.ShapeDtypeStruct((n, D_MODEL), F32),
        compiler_params=_params(("parallel", "arbitrary")),
        name="fox_prompt_attn" if fox else "moba_prompt_attn",
    )(qi, kn, *args)


def _sample_attn_body(*refs, fox, t_new, n_pages, layer):
    pt_ref, q_ref, kn_ref, vn_ref, ck_hbm, cv_hbm = refs[:6]
    if fox:
        (clf_hbm, lfn_ref, o_ref, qx, qxb, kpad, vpad, s_scr, acc, carry,
         ring, ring_sem, lf_ring, lf_sem) = refs[6:]
    else:
        o_ref, qx, qxb, kpad, vpad, s_scr, acc, kmt, ring, ring_sem = refs[6:]
    rows = t_new * N_HEADS
    past = n_pages * PAGE_SIZE
    pages_per_block = MOBA_BLOCK // PAGE_SIZE
    seq = pl.program_id(0)
    scale = HEAD_DIM ** -0.5
    row_col = lax.broadcasted_iota(jnp.int32, (rows, 1), 0)
    tok_col = row_col // N_HEADS
    head_col = row_col % N_HEADS

    def page_copy(cache_hbm, logical_page, slot):
        phys = pt_ref[seq * n_pages + logical_page]
        return pltpu.make_async_copy(cache_hbm.at[layer, phys], ring.at[slot], ring_sem.at[slot])

    def lf_copy(logical_page, slot):
        phys = pt_ref[seq * n_pages + logical_page]
        return pltpu.make_async_copy(clf_hbm.at[layer, phys], lf_ring.at[slot], lf_sem.at[slot])

    def start_fetch(f, slot):
        @pl.when(f < n_pages)
        def _():
            page_copy(ck_hbm, n_pages - 1 - f, slot).start()
            if fox:
                lf_copy(n_pages - 1 - f, slot).start()

        @pl.when(jnp.logical_and(f >= n_pages, f < 2 * n_pages))
        def _():
            page_copy(cv_hbm, f - n_pages, slot).start()

    def wait_fetch(slot, with_lf):
        page_copy(ck_hbm, 0, slot).wait()
        if with_lf:
            lf_copy(0, slot).wait()

    for f in range(RING_PAGES):
        start_fetch(jnp.int32(f), f)

    head_mask = _head_of_lane((N_HEADS, D_MODEL), 1) == lax.broadcasted_iota(jnp.int32, (N_HEADS, D_MODEL), 0)
    for t in range(t_new):
        qrow = jnp.broadcast_to(q_ref[t:t + 1, :], (N_HEADS, D_MODEL))
        qx[t * N_HEADS:(t + 1) * N_HEADS, :] = jnp.where(head_mask, qrow * scale, 0.0)
    qxb[...] = qx[...].astype(BF16)
    kpad[...] = jnp.zeros(kpad.shape, BF16)
    vpad[...] = jnp.zeros(vpad.shape, BF16)
    kpad[0:t_new, :] = kn_ref[...].astype(BF16)
    vpad[0:t_new, :] = vn_ref[...].astype(BF16)
    s_scr[:, past:past + PAGE_SIZE] = _dot_nt(qxb[...], kpad[...])
    if fox:
        carry[...] = jnp.zeros(carry.shape, F32)
    else:
        kmt[...] = jnp.zeros(kmt.shape, F32)

    def k_pages(j, _):
        f0 = j * ITER_PAGES
        slot0 = f0 % RING_PAGES
        qb = qxb[...]
        if fox:
            r0 = lax.broadcasted_iota(jnp.int32, (PAGE_SIZE, PAGE_SIZE), 0)
            r1 = lax.broadcasted_iota(jnp.int32, (PAGE_SIZE, PAGE_SIZE), 1)
            later = jnp.where(r0 > r1, 1.0, 0.0).astype(BF16)
        else:
            lane_blk = lax.broadcasted_iota(jnp.int32, (D_MODEL, LANES), 1)
            means = kmt[...]
        block_sum = None
        for d in range(ITER_PAGES):
            wait_fetch(slot0 + d, fox)
        for d in range(ITER_PAGES):
            slot = slot0 + d
            pk = n_pages - 1 - (f0 + d)
            kpage = ring[slot]
            s = _dot(qb, kpage.astype(BF16))
            if fox:
                lft = lf_ring[slot]
                suffix = _dot_split_lhs(lft, later, SPLIT_F32) + carry[...]
                carry[...] = carry[...] + jnp.sum(lft, axis=-1, keepdims=True)
                s = s + jnp.concatenate([suffix] * t_new, axis=0)
            else:
                block_sum = kpage if block_sum is None else block_sum + kpage
                if d % pages_per_block == pages_per_block - 1:
                    mean = jnp.sum(block_sum, axis=-1, keepdims=True) * (1.0 / MOBA_BLOCK)
                    means = jnp.where(lane_blk == pk // pages_per_block, mean, means)
                    block_sum = None
            s_scr[:, pl.ds(pl.multiple_of(pk * PAGE_SIZE, PAGE_SIZE), PAGE_SIZE)] = s
        if not fox:
            kmt[...] = means
        for d in range(ITER_PAGES):
            start_fetch(f0 + d + RING_PAGES, slot0 + d)
        return 0

    lax.fori_loop(0, n_pages // ITER_PAGES, k_pages, 0)

    def finalize_scores():
        chunk = 2048 if past % 2048 == 0 else PAGE_SIZE
        n_chunks = past // chunk
        lane_c = lax.broadcasted_iota(jnp.int32, (rows, chunk), 1)
        lane_p = lax.broadcasted_iota(jnp.int32, (rows, PAGE_SIZE), 1)
        own = s_scr[:, past:past + PAGE_SIZE]
        own_ok = jnp.logical_and(lane_p <= tok_col, lane_p < t_new)
        if fox:
            lane_h = lax.broadcasted_iota(jnp.int32, (N_HEADS, N_HEADS), 1)
            sub_h = lax.broadcasted_iota(jnp.int32, (N_HEADS, N_HEADS), 0)
            cum = []
            for t in range(t_new):
                lrow = jnp.broadcast_to(lfn_ref[t:t + 1, :], (N_HEADS, N_HEADS))
                col = jnp.sum(jnp.where(lane_h == sub_h, lrow, 0.0), axis=-1, keepdims=True)
                cum.append(col if t == 0 else cum[-1] + col)
            c_new = jnp.concatenate(cum, axis=0)
            c_key = jnp.zeros((rows, PAGE_SIZE), F32)
            for t in range(t_new):
                c_key = jnp.where(lane_p == t, jnp.concatenate([cum[t]] * t_new, axis=0), c_key)
            own = jnp.where(own_ok, own + (c_new - c_key), NEG)
        else:
            slope = jnp.zeros((rows, 1), F32)
            for h in range(N_HEADS):
                slope = jnp.where(head_col == h, _SLOPES[h], slope)
            t_abs = (past + tok_col).astype(F32)
            n_full = past // MOBA_BLOCK
            q_hi, q_lo = _split(qx[...], 2)
            k_hi, k_lo = _split(kmt[...], 2)
            gate = _dot(q_hi, k_hi) + (_dot(q_hi, k_lo) + _dot(q_lo, k_hi))
            gate = jnp.where(lane_p < n_full, gate, -jnp.inf)
            sel = _top_k_lanes(gate, MOBA_TOPK, n_full)
            sel_bias = jnp.where(sel, 0.0, NEG).astype(BF16)
            own = jnp.where(own_ok, own - slope * (tok_col - lane_p).astype(F32), NEG)
        s_scr[:, past:past + PAGE_SIZE] = own
        m0 = jnp.max(own, axis=-1, keepdims=True)

        def bias_pass(ci, m):
            off = pl.multiple_of(ci * chunk, chunk)
            s = s_scr[:, pl.ds(off, chunk)]
            if fox:
                s = s + c_new
            else:
                blk = lax.broadcasted_iota(jnp.int32, (LANES, chunk), 0)
                key_blk = (lax.broadcasted_iota(jnp.int32, (LANES, chunk), 1) + ci * chunk) // MOBA_BLOCK
                expand = jnp.where(blk == key_blk, 1.0, 0.0).astype(BF16)
                s = s + _dot(sel_bias, expand)
                s = s - slope * (t_abs - (lane_c + ci * chunk).astype(F32))
            s_scr[:, pl.ds(off, chunk)] = s
            return jnp.maximum(m, jnp.max(s, axis=-1, keepdims=True))

        m = lax.fori_loop(0, n_chunks, bias_pass, m0)

        def exp_pass(ci, l):
            off = pl.multiple_of(ci * chunk, chunk)
            pexp = jnp.exp(s_scr[:, pl.ds(off, chunk)] - m)
            s_scr[:, pl.ds(off, chunk)] = pexp
            return l + jnp.sum(pexp, axis=-1, keepdims=True)

        p_own = jnp.exp(own - m)
        l = lax.fori_loop(0, n_chunks, exp_pass, jnp.sum(p_own, axis=-1, keepdims=True))
        acc[...] = _dot(p_own.astype(BF16), vpad[...])
        return l

    l_sum = finalize_scores()

    def v_pages(j, _):
        f0 = n_pages + j * ITER_PAGES
        slot0 = f0 % RING_PAGES
        total = acc[...]
        for d in range(ITER_PAGES):
            wait_fetch(slot0 + d, False)
        for d in range(ITER_PAGES):
            off = pl.multiple_of((f0 + d - n_pages) * PAGE_SIZE, PAGE_SIZE)
            prob = s_scr[:, pl.ds(off, PAGE_SIZE)].astype(BF16)
            total = total + _dot_nt(prob, ring[slot0 + d].astype(BF16))
        acc[...] = total
        for d in range(ITER_PAGES):
            start_fetch(f0 + d + RING_PAGES, slot0 + d)
        return 0

    lax.fori_loop(0, n_pages // ITER_PAGES, v_pages, 0)

    out_mask = _head_of_lane((rows, D_MODEL), 1) == lax.broadcasted_iota(jnp.int32, (rows, D_MODEL), 0) % N_HEADS
    out = jnp.where(out_mask, acc[...] / l_sum, 0.0)
    for t in range(t_new):
        o_ref[t:t + 1, :] = jnp.sum(out[t * N_HEADS:(t + 1) * N_HEADS, :], axis=0, keepdims=True)


def _top_k_lanes(gate, k, n_valid):
    lane = lax.broadcasted_iota(jnp.int32, gate.shape, 1)
    sel = jnp.zeros(gate.shape, jnp.bool_)
    for r in range(k):
        mx = jnp.max(gate, axis=-1, keepdims=True)
        idx = jnp.min(jnp.where(gate == mx, lane, LANES), axis=-1, keepdims=True)
        hit = lane == idx
        sel = jnp.logical_or(sel, jnp.logical_and(hit, r < n_valid))
        gate = jnp.where(hit, -jnp.inf, gate)
    return sel


def _sample_attn(q, k_new, v_new, cache_kt, cache_vt, layer, page_table, cache_lft=None, lf_new=None):
    n_seq, t_new, _ = q.shape
    n_pages = page_table.shape[1]
    assert n_pages % RING_PAGES == 0 and ITER_PAGES % (MOBA_BLOCK // PAGE_SIZE) == 0
    assert (n_pages * PAGE_SIZE) % MOBA_BLOCK == 0 and n_pages * PAGE_SIZE // MOBA_BLOCK <= LANES
    fox = cache_lft is not None
    rows = t_new * N_HEADS
    past = n_pages * PAGE_SIZE
    seq_blk = lambda s, pt: (s, 0, 0)
    hbm = pl.BlockSpec(memory_space=pl.ANY)
    in_specs = [pl.BlockSpec((None, t_new, D_MODEL), seq_blk)] * 3 + [hbm, hbm]
    args = [q, k_new, v_new, cache_kt, cache_vt]
    scratch = [pltpu.VMEM((rows, D_MODEL), F32),
               pltpu.VMEM((rows, D_MODEL), BF16),
               pltpu.VMEM((PAGE_SIZE, D_MODEL), BF16),
               pltpu.VMEM((PAGE_SIZE, D_MODEL), BF16),
               pltpu.VMEM((rows, past + PAGE_SIZE), F32),
               pltpu.VMEM((rows, D_MODEL), F32)]
    if fox:
        in_specs += [hbm, pl.BlockSpec((None, t_new, N_HEADS), seq_blk)]
        args += [cache_lft, lf_new]
        scratch += [pltpu.VMEM((N_HEADS, 1), F32)]
    else:
        scratch += [pltpu.VMEM((D_MODEL, LANES), F32)]
    scratch += [pltpu.VMEM((RING_PAGES, D_MODEL, PAGE_SIZE), F32), pltpu.SemaphoreType.DMA((RING_PAGES,))]
    if fox:
        scratch += [pltpu.VMEM((RING_PAGES, N_HEADS, PAGE_SIZE), F32), pltpu.SemaphoreType.DMA((RING_PAGES,))]
    return pl.pallas_call(
        functools.partial(_sample_attn_body, fox=fox, t_new=t_new, n_pages=n_pages, layer=layer),
        grid_spec=pltpu.PrefetchScalarGridSpec(
            num_scalar_prefetch=1,
            grid=(n_seq,),
            in_specs=in_specs,
            out_specs=pl.BlockSpec((None, t_new, D_MODEL), seq_blk),
            scratch_shapes=scratch),
        out_shape=jax.ShapeDtypeStruct((n_seq, t_new, D_MODEL), F32),
        compiler_params=_params(("arbitrary",)),
        name="fox_sample_attn" if fox else "moba_sample_attn",
    )(page_table.reshape(-1), *args)


def kernel(x_prompt, x_sample, cache_k_moba, cache_v_moba, cache_k_fox, cache_v_fox, cache_logf_fox, state_conv, page_table, attn_norm, ffn_norm, moba_w_qkv, moba_q_gain, moba_k_gain, moba_w_o, fox_w_qkvf, fox_b_f, fox_q_gain, fox_k_gain, fox_w_o, ffn_w_gate, ffn_w_up, ffn_conv_w, ffn_conv_b, ffn_w_down):
    b, n, d = x_prompt.shape
    db, t_new, _ = x_sample.shape
    depth = attn_norm.shape[0]
    f = ffn_w_gate.shape[-1]
    page_t = lambda a: a.transpose(0, 1, 3, 4, 2).reshape(a.shape[0], a.shape[1], D_MODEL, PAGE_SIZE)
    cache_k_moba, cache_v_moba = page_t(cache_k_moba), page_t(cache_v_moba)
    cache_k_fox, cache_v_fox = page_t(cache_k_fox), page_t(cache_v_fox)
    cache_lft = cache_logf_fox.transpose(0, 1, 3, 2)
    heads = lambda a, lead: a.reshape(lead + (N_HEADS, HEAD_DIM))
    heads_t = lambda a: a.reshape(b, N_HEADS, HEAD_DIM, n).transpose(0, 3, 1, 2)
    seq3 = lambda a: a.reshape(db, t_new, a.shape[-1])

    yp = x_prompt.reshape(b * n, d)
    ys = x_sample.reshape(db * t_new, d)
    moba_out = [[] for _ in range(4)]
    fox_out = [[] for _ in range(6)]
    conv_p, conv_s = [], []
    zero_hist = jnp.zeros((b, CONV_WIDTH - 1, f), F32)
    for layer in range(depth):
        j = layer // 2
        if layer % 2 == 0:
            w = moba_w_qkv[j].astype(BF16)
            qp, kt, ktb, vt, vtb, kmean = _qkv_prompt(yp, attn_norm[layer], w, moba_q_gain[j], moba_k_gain[j], batch=b)
            qs, ks, vs = _qkv_rows(ys, attn_norm[layer], w, moba_q_gain[j], moba_k_gain[j])
            mix_p = _prompt_attn(qp, ktb, vtb, kmean=kmean.reshape(b, n // MOBA_BLOCK, d))
            mix_s = _sample_attn(seq3(qs), seq3(ks), seq3(vs), cache_k_moba, cache_v_moba, j, page_table)
            w_o = moba_w_o[j]
            for lst, a in zip(moba_out, (heads_t(kt), heads_t(vt), heads(ks, (db, t_new)), heads(vs, (db, t_new)))):
                lst.append(a)
        else:
            w = fox_w_qkvf[j, :, :3 * d].astype(BF16)
            wf = jnp.pad(fox_w_qkvf[j, :, 3 * d:], ((0, 0), (0, LANES - N_HEADS))).astype(BF16)
            bf = jnp.pad(fox_b_f[j], (0, LANES - N_HEADS)).reshape(1, LANES)
            qp, kt, ktb, vt, vtb, lft = _qkv_prompt(yp, attn_norm[layer], w, fox_q_gain[j], fox_k_gain[j],
                                                    batch=b, wf=wf, bf=bf)
            qs, ks, vs, ls = _qkv_rows(ys, attn_norm[layer], w, fox_q_gain[j], fox_k_gain[j], wf, bf)
            mix_p = _prompt_attn(qp, ktb, vtb, ct=_seq_cumsum(lft))
            mix_s = _sample_attn(seq3(qs), seq3(ks), seq3(vs), cache_k_fox, cache_v_fox, j, page_table,
                                 cache_lft=cache_lft, lf_new=seq3(ls))
            w_o = fox_w_o[j]
            for lst, a in zip(fox_out, (heads_t(kt), heads_t(vt), lft.transpose(0, 2, 1),
                                        heads(ks, (db, t_new)), heads(vs, (db, t_new)), seq3(ls))):
                lst.append(a)
        w_o = w_o.astype(BF16)
        yp = _wo_proj(yp, mix_p, w_o)
        ys = _wo_proj(ys, mix_s.reshape(db * t_new, d), w_o)
        ffn_w = (ffn_norm[layer], ffn_w_gate[layer].astype(BF16), ffn_w_up[layer].astype(BF16),
                 ffn_conv_w[layer], ffn_conv_b[layer], ffn_w_down[layer].astype(BF16))
        yp, cp = _conv_ffn(yp, zero_hist, *ffn_w, rows_per_seq=n)
        ys, cs = _conv_ffn(ys, state_conv[layer], *ffn_w, rows_per_seq=t_new)
        conv_p.append(cp)
        conv_s.append(cs)
    return (yp.reshape(b, n, d), ys.reshape(db, t_new, d),
            *[jnp.stack(a) for a in moba_out],
            *[jnp.stack(a) for a in fox_out],
            jnp.stack(conv_p), jnp.stack(conv_s))
```
